```python
import jax, jax.numpy as jnp
from jax import lax
import numpy as np

D_MODEL = 1024
BATCH = 2
SEQ = 8192
DEPTH = 1
DEC_BATCH = 8
DEC_SEQ = 2048
PAST_LEN = 128

CHUNK = 128
A_WIDTH = 1024
A_GROUPS = 8
A_GROUP_DIM = A_WIDTH // A_GROUPS
N_HEADS = 8
QK_NOPE_DIM = 128
QK_ROPE_DIM = 64
QK_HEAD_DIM = QK_NOPE_DIM + QK_ROPE_DIM
V_HEAD_DIM = 128
Q_LORA_RANK = 256
KV_LORA_RANK = 128
ROPE_THETA = 10000.0
Q_BLOCK = 128
N_EXPERTS = 16
EXPERT_FF = 1024
CAPACITY_FACTOR = 2
NORM_EPS = 1e-6

OFF_UV = 0
OFF_CQ = OFF_UV + 2 * A_WIDTH
OFF_CKV = OFF_CQ + Q_LORA_RANK
OFF_KR = OFF_CKV + KV_LORA_RANK
OFF_GA = OFF_KR + QK_ROPE_DIM
OFF_GB = OFF_GA + D_MODEL
IN_COLS = OFF_GB + D_MODEL

kernel_name = "hybrid_gmlp_mla_ec_encoder"


def _rmsnorm(x, g):
    xf = x.astype(jnp.float32)
    y = xf * lax.rsqrt(jnp.mean(xf * xf, axis=-1, keepdims=True) + NORM_EPS)
    return (y * g.astype(jnp.float32)).astype(x.dtype)


def _rope_tables(seq):
    half = QK_ROPE_DIM // 2
    inv_freq = ROPE_THETA ** (-jnp.arange(half, dtype=jnp.float32) / half)
    freqs = jnp.arange(seq, dtype=jnp.float32)[:, None] * inv_freq[None, :]
    return jnp.cos(freqs)[:, None, :], jnp.sin(freqs)[:, None, :]


def _rope(x, cos, sin):
    half = QK_ROPE_DIM // 2
    xf = x.astype(jnp.float32)
    x1, x2 = xf[..., :half], xf[..., half:]
    return jnp.concatenate([x1 * cos - x2 * sin, x1 * sin + x2 * cos], axis=-1).astype(x.dtype)


def _chunk_gmlp(uv, v_norm_g, w_spatial, b_spatial, w_a_out):
    B, S, _ = uv.shape
    uv = jax.nn.gelu(uv, approximate=False)
    u, v = uv[..., :A_WIDTH], uv[..., A_WIDTH:]
    v = _rmsnorm(v, v_norm_g)
    v = v.reshape(B, S // CHUNK, CHUNK, A_GROUPS, A_GROUP_DIM)
    sv = jnp.einsum('gpq,bnqgc->bnpgc', w_spatial, v) + b_spatial.T[None, None, :, :, None]
    sv = sv.reshape(B, S, A_WIDTH)
    return (u * sv) @ w_a_out


def _mla(c_q, c_kv, k_rope, q_norm_g, w_uq, kv_norm_g, w_ukv, q_head_g, k_head_g, w_b_out):
    B, S, _ = c_q.shape
    cos, sin = _rope_tables(S)
    q = (_rmsnorm(c_q, q_norm_g) @ w_uq).reshape(B, S, N_HEADS, QK_HEAD_DIM)
    kv = (_rmsnorm(c_kv, kv_norm_g) @ w_ukv).reshape(B, S, N_HEADS, QK_NOPE_DIM + V_HEAD_DIM)
    k_nope, v = kv[..., :QK_NOPE_DIM], kv[..., QK_NOPE_DIM:]
    k_r = jnp.broadcast_to(k_rope[:, :, None, :], (B, S, N_HEADS, QK_ROPE_DIM))
    k = jnp.concatenate([k_nope, k_r], axis=-1)
    q = _rmsnorm(q, q_head_g)
    k = _rmsnorm(k, k_head_g)
    q = jnp.concatenate([q[..., :QK_NOPE_DIM], _rope(q[..., QK_NOPE_DIM:], cos, sin)], axis=-1)
    k = jnp.concatenate([k[..., :QK_NOPE_DIM], _rope(k[..., QK_NOPE_DIM:], cos, sin)], axis=-1)
    scale = QK_HEAD_DIM ** -0.5
    n_blk = S // Q_BLOCK
    q_blocks = q.reshape(B, n_blk, Q_BLOCK, N_HEADS, QK_HEAD_DIM).transpose(1, 0, 2, 3, 4)

    def attend(qb):
        s = jnp.einsum('bqhd,bkhd->bhqk', qb, k, preferred_element_type=jnp.float32) * scale
        p = jax.nn.softmax(s, axis=-1).astype(v.dtype)
        return jnp.einsum('bhqk,bkhd->bqhd', p, v)

    o = lax.map(attend, q_blocks)
    o = o.transpose(1, 0, 2, 3, 4).reshape(B, S, N_HEADS * V_HEAD_DIM)
    return o @ w_b_out


def _expert_choice_ffn(h, w_router, w_gate, w_up, w_down):
    B, S, D = h.shape
    n_tok = B * S
    t = h.reshape(n_tok, D)
    cap = max(1, min(n_tok, CAPACITY_FACTOR * n_tok // N_EXPERTS))
    aff = jax.nn.softmax((t @ w_router).astype(jnp.float32), axis=-1)
    gates, idx = lax.top_k(aff.T, cap)
    xs = t[idx]
    hid = jax.nn.silu(jnp.einsum('ecd,edf->ecf', xs, w_gate)) * jnp.einsum('ecd,edf->ecf', xs, w_up)
    out = jnp.einsum('ecf,efd->ecd', hid, w_down) * gates[..., None].astype(t.dtype)
    y = jnp.zeros_like(t).at[idx.reshape(-1)].add(out.reshape(-1, D))
    return y.reshape(B, S, D)


def _layer(x, norm1_g, w_in, v_norm_g, w_spatial, b_spatial, w_a_out, q_norm_g, w_uq,
           kv_norm_g, w_ukv, q_head_g, k_head_g, w_b_out, w_out, norm2_g,
           w_router, w_gate, w_up, w_down):
    h = _rmsnorm(x, norm1_g)
    p = h @ w_in
    uv = p[..., OFF_UV:OFF_CQ]
    c_q = p[..., OFF_CQ:OFF_CKV]
    c_kv = p[..., OFF_CKV:OFF_KR]
    k_rope = p[..., OFF_KR:OFF_GA]
    g_a = jax.nn.sigmoid(p[..., OFF_GA:OFF_GB])
    g_b = jax.nn.sigmoid(p[..., OFF_GB:IN_COLS])
    a_out = _chunk_gmlp(uv, v_norm_g, w_spatial, b_spatial, w_a_out)
    b_out = _mla(c_q, c_kv, k_rope, q_norm_g, w_uq, kv_norm_g, w_ukv, q_head_g, k_head_g, w_b_out)
    x = x + (g_a * a_out + g_b * b_out) @ w_out
    x = x + _expert_choice_ffn(_rmsnorm(x, norm2_g), w_router, w_gate, w_up, w_down)
    return x


def setup_inputs(seed: int = 0) -> dict:
    key = jax.random.key(seed)
    ks = jax.random.split(key, 24)

    def nrm(k, shape, fan_in):
        return jax.random.normal(k, shape, jnp.float32) * (fan_in ** -0.5)

    def gain(k, shape):
        return 1.0 + 0.02 * jax.random.normal(k, shape, jnp.float32)

    L = DEPTH
    return {
        "x_prompt": jax.random.normal(ks[0], (BATCH, SEQ, D_MODEL), jnp.float32),
        "x_sample": jax.random.normal(ks[1], (DEC_BATCH, DEC_SEQ, D_MODEL), jnp.float32),
        "norm1_g": gain(ks[2], (L, D_MODEL)),
        "w_in": nrm(ks[3], (L, D_MODEL, IN_COLS), D_MODEL),
        "v_norm_g": gain(ks[4], (L, A_WIDTH)),
        "w_spatial": nrm(ks[5], (L, A_GROUPS, CHUNK, CHUNK), CHUNK),
        "b_spatial": gain(ks[6], (L, A_GROUPS, CHUNK)),
        "w_a_out": nrm(ks[7], (L, A_WIDTH, D_MODEL), A_WIDTH),
        "q_norm_g": gain(ks[8], (L, Q_LORA_RANK)),
        "w_uq": nrm(ks[9], (L, Q_LORA_RANK, N_HEADS * QK_HEAD_DIM), Q_LORA_RANK),
        "kv_norm_g": gain(ks[10], (L, KV_LORA_RANK)),
        "w_ukv": nrm(ks[11], (L, KV_LORA_RANK, N_HEADS * (QK_NOPE_DIM + V_HEAD_DIM)), KV_LORA_RANK),
        "q_head_g": gain(ks[12], (L, QK_HEAD_DIM)),
        "k_head_g": gain(ks[13], (L, QK_HEAD_DIM)),
        "w_b_out": nrm(ks[14], (L, N_HEADS * V_HEAD_DIM, D_MODEL), N_HEADS * V_HEAD_DIM),
        "w_out": nrm(ks[15], (L, D_MODEL, D_MODEL), D_MODEL),
        "norm2_g": gain(ks[16], (L, D_MODEL)),
        "w_router": nrm(ks[17], (L, D_MODEL, N_EXPERTS), D_MODEL),
        "w_gate": nrm(ks[18], (L, N_EXPERTS, D_MODEL, EXPERT_FF), D_MODEL),
        "w_up": nrm(ks[19], (L, N_EXPERTS, D_MODEL, EXPERT_FF), D_MODEL),
        "w_down": nrm(ks[20], (L, N_EXPERTS, EXPERT_FF, D_MODEL), EXPERT_FF),
    }


def reference(x_prompt, x_sample, norm1_g, w_in, v_norm_g, w_spatial, b_spatial, w_a_out,
              q_norm_g, w_uq, kv_norm_g, w_ukv, q_head_g, k_head_g, w_b_out, w_out,
              norm2_g, w_router, w_gate, w_up, w_down):
    params = (norm1_g, w_in, v_norm_g, w_spatial, b_spatial, w_a_out, q_norm_g, w_uq,
              kv_norm_g, w_ukv, q_head_g, k_head_g, w_b_out, w_out, norm2_g,
              w_router, w_gate, w_up, w_down)

    def run(x):
        for l in range(DEPTH):
            x = _layer(x, *[prm[l] for prm in params])
        return x

    y_prompt = run(x_prompt)
    y_sample = run(x_sample)
    return (y_prompt, y_sample)
```

```python
import functools
import math

import jax
import jax.numpy as jnp
from jax import lax
from jax.experimental import pallas as pl
from jax.experimental.pallas import tpu as pltpu

D_MODEL = 1024
CHUNK = 128
A_WIDTH = 1024
A_GROUPS = 8
A_GROUP_DIM = A_WIDTH // A_GROUPS
N_HEADS = 8
QK_NOPE_DIM = 128
QK_ROPE_DIM = 64
QK_HEAD_DIM = QK_NOPE_DIM + QK_ROPE_DIM
V_HEAD_DIM = 128
Q_LORA_RANK = 256
KV_LORA_RANK = 128
ROPE_THETA = 10000.0
N_EXPERTS = 16
EXPERT_FF = 1024
CAPACITY_FACTOR = 2
NORM_EPS = 1e-6

LANES = 128
BF16_SUBLANES = 16
VMEM_LIMIT_BYTES = 56 * 1024 * 1024

QK_PAD = 2 * LANES
HALF_ROPE = QK_ROPE_DIM // 2

P_UV = 0
P_SMALL = P_UV + 2 * A_WIDTH
P_SMALL_W = Q_LORA_RANK + KV_LORA_RANK + 2 * QK_ROPE_DIM
P_GA = P_SMALL + P_SMALL_W
P_GB = P_GA + D_MODEL
P_COLS = P_GB + D_MODEL

PRE_TM = 256
POST_TM = 512
ATT_TQ = 512
ATT_TK = 512
TOK_TILE = 128
SLOT_BLOCK = 256
FFN_TM = 512
READ_WINDOW = TOK_TILE + BF16_SUBLANES

BF16 = jnp.bfloat16
F32 = jnp.float32


def _cparams(sem):
    return pltpu.CompilerParams(dimension_semantics=sem, vmem_limit_bytes=VMEM_LIMIT_BYTES)


def _rms(x, n):
    return lax.rsqrt(jnp.sum(x * x, axis=-1, keepdims=True) * (1.0 / n) + NORM_EPS)


def _const_spec(shape):
    nd = len(shape)
    return pl.BlockSpec(shape, lambda *_: (0,) * nd)


def _pre_kernel(x_ref, cs_ref, g1_ref, win_ref, vg_ref, ws_ref, bt_ref, wa_ref,
                qng_ref, wuq_ref, kvg_ref, wukv_ref, qg_ref, kgn_ref, kg2_ref,
                ab_ref, gb_ref, q_ref, k_ref, v_ref, sv_ref):
    x = x_ref[...]
    h = (x * _rms(x, D_MODEL) * g1_ref[...]).astype(BF16)

    uv = jnp.dot(h, win_ref[:, P_UV:P_SMALL], preferred_element_type=F32)
    uv = 0.5 * uv * (1.0 + lax.erf(uv * (1.0 / math.sqrt(2.0))))
    u = uv[:, :A_WIDTH]
    v = uv[:, A_WIDTH:]
    vn = (v * _rms(v, A_WIDTH) * vg_ref[...]).astype(BF16)
    for n in range(PRE_TM // CHUNK):
        for g in range(A_GROUPS):
            blk = vn[n * CHUNK:(n + 1) * CHUNK, g * A_GROUP_DIM:(g + 1) * A_GROUP_DIM]
            sv = jnp.dot(ws_ref[g], blk, preferred_element_type=F32) + bt_ref[:, g:g + 1]
            sv_ref[n * CHUNK:(n + 1) * CHUNK, g * A_GROUP_DIM:(g + 1) * A_GROUP_DIM] = sv
    a = jnp.dot((u * sv_ref[...]).astype(BF16), wa_ref[...], preferred_element_type=F32)
    ga = jax.nn.sigmoid(jnp.dot(h, win_ref[:, P_GA:P_GB], preferred_element_type=F32))
    ab_ref[...] = ga * a
    gb_ref[...] = jax.nn.sigmoid(jnp.dot(h, win_ref[:, P_GB:P_COLS], preferred_element_type=F32))

    small = jnp.dot(h, win_ref[:, P_SMALL:P_GA], preferred_element_type=F32)
    cq = small[:, :Q_LORA_RANK]
    ckv = small[:, Q_LORA_RANK:Q_LORA_RANK + KV_LORA_RANK]
    kr = small[:, Q_LORA_RANK + KV_LORA_RANK:]
    cs = cs_ref[...]
    lane = lax.broadcasted_iota(jnp.int32, (PRE_TM, LANES), 1)
    rope_lanes = lane < QK_ROPE_DIM

    cqn = (cq * _rms(cq, Q_LORA_RANK) * qng_ref[...]).astype(BF16)
    qf = jnp.dot(cqn, wuq_ref[...], preferred_element_type=F32)
    ckvn = (ckv * _rms(ckv, KV_LORA_RANK) * kvg_ref[...]).astype(BF16)
    kvf = jnp.dot(ckvn, wukv_ref[...], preferred_element_type=F32)

    kr_ss = jnp.sum(jnp.where(rope_lanes, kr * kr, 0.0), axis=-1, keepdims=True)
    t = kr * kg2_ref[...] * cs
    kro = jnp.where(rope_lanes, t + pltpu.roll(t, QK_ROPE_DIM, 1), 0.0)

    qg = qg_ref[...]
    scale = QK_HEAD_DIM ** -0.5
    for hd in range(N_HEADS):
        qn = qf[:, hd * QK_PAD:hd * QK_PAD + LANES]
        q2 = qf[:, hd * QK_PAD + LANES:(hd + 1) * QK_PAD]
        q_ss = (jnp.sum(qn * qn, axis=-1, keepdims=True)
                + jnp.sum(jnp.where(rope_lanes, q2 * q2, 0.0), axis=-1, keepdims=True))
        q_rs = lax.rsqrt(q_ss * (1.0 / QK_HEAD_DIM) + NORM_EPS) * scale
        t = q2 * qg[:, LANES:] * cs
        q_ref[0, hd, :, :LANES] = (qn * qg[:, :LANES] * q_rs).astype(BF16)
        q_ref[0, hd, :, LANES:] = ((t + pltpu.roll(t, QK_ROPE_DIM, 1)) * q_rs).astype(BF16)

        kn = kvf[:, hd * QK_NOPE_DIM:(hd + 1) * QK_NOPE_DIM]
        k_ss = jnp.sum(kn * kn, axis=-1, keepdims=True) + kr_ss
        k_rs = lax.rsqrt(k_ss * (1.0 / QK_HEAD_DIM) + NORM_EPS)
        k_ref[0, hd, :, :LANES] = (kn * kgn_ref[...] * k_rs).astype(BF16)
        k_ref[0, hd, :, LANES:] = (kro * k_rs).astype(BF16)
        v_ref[0, hd] = kvf[:, N_HEADS * QK_NOPE_DIM + hd * V_HEAD_DIM:
                           N_HEADS * QK_NOPE_DIM + (hd + 1) * V_HEAD_DIM].astype(BF16)


def _pre(x2d, cs, w, batch, seq):
    n_tok = batch * seq
    tiles_per_seq = seq // PRE_TM
    tok = lambda i: (i, 0)
    head = lambda i: (i // tiles_per_seq, 0, i % tiles_per_seq, 0)
    in_specs = [
        pl.BlockSpec((PRE_TM, D_MODEL), tok),
        pl.BlockSpec((PRE_TM, LANES), lambda i: (i % tiles_per_seq, 0)),
        _const_spec((1, D_MODEL)),
        _const_spec((D_MODEL, P_COLS)),
        _const_spec((1, A_WIDTH)),
        _const_spec((A_GROUPS, CHUNK, CHUNK)),
        _const_spec((CHUNK, A_GROUPS)),
        _const_spec((A_WIDTH, D_MODEL)),
        _const_spec((1, Q_LORA_RANK)),
        _const_spec((Q_LORA_RANK, N_HEADS * QK_PAD)),
        _const_spec((1, KV_LORA_RANK)),
        _const_spec((KV_LORA_RANK, N_HEADS * (QK_NOPE_DIM + V_HEAD_DIM))),
        _const_spec((1, QK_PAD)),
        _const_spec((1, LANES)),
        _const_spec((1, LANES)),
    ]
    out_specs = [
        pl.BlockSpec((PRE_TM, D_MODEL), tok),
        pl.BlockSpec((PRE_TM, D_MODEL), tok),
        pl.BlockSpec((1, N_HEADS, PRE_TM, QK_PAD), head),
        pl.BlockSpec((1, N_HEADS, PRE_TM, QK_PAD), head),
        pl.BlockSpec((1, N_HEADS, PRE_TM, V_HEAD_DIM), head),
    ]
    out_shape = [
        jax.ShapeDtypeStruct((n_tok, D_MODEL), F32),
        jax.ShapeDtypeStruct((n_tok, D_MODEL), F32),
        jax.ShapeDtypeStruct((batch, N_HEADS, seq, QK_PAD), BF16),
        jax.ShapeDtypeStruct((batch, N_HEADS, seq, QK_PAD), BF16),
        jax.ShapeDtypeStruct((batch, N_HEADS, seq, V_HEAD_DIM), BF16),
    ]
    return pl.pallas_call(
        _pre_kernel,
        grid=(n_tok // PRE_TM,),
        in_specs=in_specs,
        out_specs=out_specs,
        out_shape=out_shape,
        scratch_shapes=[pltpu.VMEM((PRE_TM, A_WIDTH), F32)],
        compiler_params=_cparams(("arbitrary",)),
        name="pre",
    )(x2d, cs, w["g1"], w["w_in"], w["vg"], w["ws"], w["bt"], w["wa"], w["qng"], w["wuq"],
      w["kvg"], w["wukv"], w["qg"], w["kgn"], w["kg2"])


def _attn_kernel(q_ref, k_ref, v_ref, o_ref, m_ref, l_ref, acc_ref, *, seq):
    q = q_ref[0, 0]
    m_ref[...] = jnp.full(m_ref.shape, -jnp.inf, F32)
    l_ref[...] = jnp.zeros(l_ref.shape, F32)
    acc_ref[...] = jnp.zeros(acc_ref.shape, F32)

    def body(j, carry):
        start = pl.multiple_of(j * ATT_TK, ATT_TK)
        kt = k_ref[0, 0, pl.ds(start, ATT_TK), :]
        vt = v_ref[0, 0, pl.ds(start, ATT_TK), :]
        s = lax.dot_general(q, kt, (((1,), (1,)), ((), ())), preferred_element_type=F32)
        m_old = m_ref[...]
        m_new = jnp.maximum(m_old, jnp.max(s, axis=-1, keepdims=True))
        alpha = jnp.exp(m_old - m_new)
        p = jnp.exp(s - m_new)
        l_ref[...] = alpha * l_ref[...] + jnp.sum(p, axis=-1, keepdims=True)
        acc_ref[...] = alpha * acc_ref[...] + jnp.dot(p.astype(BF16), vt, preferred_element_type=F32)
        m_ref[...] = m_new
        return carry

    lax.fori_loop(0, seq // ATT_TK, body, 0)
    o_ref[0] = (acc_ref[...] / l_ref[...]).astype(BF16)


def _attn(q, k, v, batch, seq):
    return pl.pallas_call(
        functools.partial(_attn_kernel, seq=seq),
        grid=(batch, N_HEADS, seq // ATT_TQ),
        in_specs=[
            pl.BlockSpec((1, 1, ATT_TQ, QK_PAD), lambda b, h, i: (b, h, i, 0)),
            pl.BlockSpec((1, 1, seq, QK_PAD), lambda b, h, i: (b, h, 0, 0)),
            pl.BlockSpec((1, 1, seq, V_HEAD_DIM), lambda b, h, i: (b, h, 0, 0)),
        ],
        out_specs=pl.BlockSpec((1, ATT_TQ, V_HEAD_DIM), lambda b, h, i: (b, i, h)),
        out_shape=jax.ShapeDtypeStruct((batch, seq, N_HEADS * V_HEAD_DIM), BF16),
        scratch_shapes=[pltpu.VMEM((ATT_TQ, 1), F32), pltpu.VMEM((ATT_TQ, 1), F32),
                        pltpu.VMEM((ATT_TQ, V_HEAD_DIM), F32)],
        compiler_params=_cparams(("arbitrary", "arbitrary", "arbitrary")),
        name="attn",
    )(q, k, v)


def _post_kernel(x_ref, o_ref, ab_ref, gb_ref, wb_ref, wo_ref, g2_ref, wr_ref,
                 x1_ref, hn_ref, aff_ref, aff16_ref):
    b = jnp.dot(o_ref[...], wb_ref[...], preferred_element_type=F32)
    mix = ab_ref[...] + gb_ref[...] * b
    x1 = x_ref[...] + jnp.dot(mix.astype(BF16), wo_ref[...], preferred_element_type=F32)
    x1_ref[...] = x1
    hn = (x1 * _rms(x1, D_MODEL) * g2_ref[...]).astype(BF16)
    hn_ref[...] = hn
    logits = jnp.dot(hn, wr_ref[...], preferred_element_type=F32)
    lane = lax.broadcasted_iota(jnp.int32, logits.shape, 1)
    logits = jnp.where(lane < N_EXPERTS, logits, -jnp.inf)
    e = jnp.exp(logits - jnp.max(logits, axis=-1, keepdims=True))
    aff = e / jnp.sum(e, axis=-1, keepdims=True)
    aff_ref[...] = aff
    aff16_ref[...] = aff[:, :N_EXPERTS]


def _post(x2d, o2d, ab, gb, w):
    n_tok = x2d.shape[0]
    tok = lambda i: (i, 0)
    return pl.pallas_call(
        _post_kernel,
        grid=(n_tok // POST_TM,),
        in_specs=[
            pl.BlockSpec((POST_TM, D_MODEL), tok),
            pl.BlockSpec((POST_TM, D_MODEL), tok),
            pl.BlockSpec((POST_TM, D_MODEL), tok),
            pl.BlockSpec((POST_TM, D_MODEL), tok),
            _const_spec((N_HEADS * V_HEAD_DIM, D_MODEL)),
            _const_spec((D_MODEL, D_MODEL)),
            _const_spec((1, D_MODEL)),
            _const_spec((D_MODEL, LANES)),
        ],
        out_specs=[
            pl.BlockSpec((POST_TM, D_MODEL), tok),
            pl.BlockSpec((POST_TM, D_MODEL), tok),
            pl.BlockSpec((POST_TM, LANES), tok),
            pl.BlockSpec((POST_TM, N_EXPERTS), tok),
        ],
        out_shape=[
            jax.ShapeDtypeStruct((n_tok, D_MODEL), F32),
            jax.ShapeDtypeStruct((n_tok, D_MODEL), BF16),
            jax.ShapeDtypeStruct((n_tok, LANES), F32),
            jax.ShapeDtypeStruct((n_tok, N_EXPERTS), F32),
        ],
        compiler_params=_cparams(("arbitrary",)),
        name="post",
    )(x2d, o2d, ab, gb, w["wb"], w["wo"], w["g2"], w["wr"])


def _thresh_kernel(aff_ref, thr_ref, need_ref, *, cap):
    bits = pltpu.bitcast(aff_ref[...], jnp.int32)

    def per_expert(c):
        for shift in (64, 32, 16):
            c = c + pltpu.roll(c, shift, 1)
        return c

    def body(i, cur):
        cand = cur | jnp.left_shift(jnp.int32(1), 30 - i)
        cnt = per_expert(jnp.sum(jnp.where(bits >= cand, 1.0, 0.0), axis=0, keepdims=True))
        return jnp.where(cnt >= cap, cand, cur)

    cur = lax.fori_loop(0, 31, body, jnp.zeros((1, LANES), jnp.int32))
    n_gt = per_expert(jnp.sum(jnp.where(bits > cur, 1.0, 0.0), axis=0, keepdims=True))
    thr_ref[...] = pltpu.bitcast(cur, F32)
    need_ref[...] = cap - n_gt


def _thresh(aff16, cap):
    n_tok = aff16.shape[0]
    packed = aff16.reshape(n_tok * N_EXPERTS // LANES, LANES)
    thr, need = pl.pallas_call(
        functools.partial(_thresh_kernel, cap=float(cap)),
        out_shape=[jax.ShapeDtypeStruct((1, LANES), F32), jax.ShapeDtypeStruct((1, LANES), F32)],
        compiler_params=pltpu.CompilerParams(vmem_limit_bytes=VMEM_LIMIT_BYTES),
        name="thresh",
    )(packed)
    lane = jnp.arange(LANES)
    thr = jnp.where(lane < N_EXPERTS, thr, jnp.inf)
    need = jnp.where(lane < N_EXPERTS, need, 0.0)
    return thr, need


def _slots_kernel(aff_ref, thr_ref, need_ref, lp_tok_ref, lp_exp_ref, off_ref, run_sel, run_eq):
    @pl.when(pl.program_id(0) == 0)
    def _():
        run_sel[...] = jnp.zeros(run_sel.shape, F32)
        run_eq[...] = jnp.zeros(run_eq.shape, F32)

    aff = aff_ref[...]
    thr = thr_ref[...]
    row = lax.broadcasted_iota(jnp.int32, (TOK_TILE, TOK_TILE), 0)
    col = lax.broadcasted_iota(jnp.int32, (TOK_TILE, TOK_TILE), 1)
    earlier = jnp.where(col < row, 1.0, 0.0).astype(BF16)
    eq = aff == thr
    eqf = jnp.where(eq, 1.0, 0.0)
    eq_rank = jnp.dot(earlier, eqf.astype(BF16), preferred_element_type=F32) + run_eq[...]
    sel = (aff > thr) | (eq & (eq_rank < need_ref[...]))
    self_ = jnp.where(sel, 1.0, 0.0)
    lpos = jnp.dot(earlier, self_.astype(BF16), preferred_element_type=F32)
    lp = jnp.where(sel, lpos, -1.0)
    lp_tok_ref[...] = lp
    lp_exp_ref[0] = lp.T[:N_EXPERTS, :]
    off_ref[0] = run_sel[...]
    run_sel[...] = run_sel[...] + jnp.sum(self_, axis=0, keepdims=True)
    run_eq[...] = run_eq[...] + jnp.sum(eqf, axis=0, keepdims=True)


def _slots(aff, thr, need):
    n_tok = aff.shape[0]
    n_tiles = n_tok // TOK_TILE
    return pl.pallas_call(
        _slots_kernel,
        grid=(n_tiles,),
        in_specs=[pl.BlockSpec((TOK_TILE, LANES), lambda i: (i, 0)),
                  _const_spec((1, LANES)), _const_spec((1, LANES))],
        out_specs=[pl.BlockSpec((TOK_TILE, LANES), lambda i: (i, 0)),
                   pl.BlockSpec((1, N_EXPERTS, TOK_TILE), lambda i: (i, 0, 0)),
                   pl.BlockSpec((1, 1, LANES), lambda i: (i, 0, 0))],
        out_shape=[jax.ShapeDtypeStruct((n_tok, LANES), F32),
                   jax.ShapeDtypeStruct((n_tiles, N_EXPERTS, TOK_TILE), F32),
                   jax.ShapeDtypeStruct((n_tiles, 1, LANES), F32)],
        scratch_shapes=[pltpu.VMEM((1, LANES), F32), pltpu.VMEM((1, LANES), F32)],
        compiler_params=_cparams(("arbitrary",)),
        name="slots",
    )(aff, thr, need)


def _gather_kernel(off_ref, lo_ref, hi_ref, lp_ref, hn_hbm, xs_ref, buf, sem, acc_ref, *, n_blocks):
    e = pl.program_id(0)
    j = pl.program_id(1)
    lo = lo_ref[e * n_blocks + j]
    hi = hi_ref[e * n_blocks + j]

    def tile_copy(c, slot):
        start = pl.multiple_of(c * TOK_TILE, TOK_TILE)
        return pltpu.make_async_copy(hn_hbm.at[pl.ds(start, TOK_TILE), :], buf.at[slot], sem.at[slot])

    tile_copy(lo, 0).start()
    acc_ref[...] = jnp.zeros(acc_ref.shape, F32)
    slot_id = lax.broadcasted_iota(jnp.int32, (SLOT_BLOCK, TOK_TILE), 0).astype(F32)
    base = (j * SLOT_BLOCK).astype(F32)

    def body(c, carry):
        slot = (c - lo) % 2
        tile_copy(c, slot).wait()

        @pl.when(c < hi)
        def _():
            tile_copy(c + 1, 1 - slot).start()

        lp = lp_ref[c, pl.ds(e, 1), :]
        dst = lp + (off_ref[c * N_EXPERTS + e].astype(F32) - base)
        onehot = jnp.where((slot_id == dst) & (lp >= 0.0), 1.0, 0.0).astype(BF16)
        acc_ref[...] += jnp.dot(onehot, buf[slot], preferred_element_type=F32)
        return carry

    lax.fori_loop(lo, hi + 1, body, 0)
    xs_ref[0] = acc_ref[...].astype(BF16)


def _gather(off_flat, lo, hi, lp_exp, hn, cap):
    n_tiles = lp_exp.shape[0]
    n_blocks = cap // SLOT_BLOCK
    grid_spec = pltpu.PrefetchScalarGridSpec(
        num_scalar_prefetch=3,
        grid=(N_EXPERTS, n_blocks),
        in_specs=[pl.BlockSpec((n_tiles, N_EXPERTS, TOK_TILE), lambda e, j, *_: (0, 0, 0)),
                  pl.BlockSpec(memory_space=pl.ANY)],
        out_specs=pl.BlockSpec((1, SLOT_BLOCK, D_MODEL), lambda e, j, *_: (e, j, 0)),
        scratch_shapes=[pltpu.VMEM((2, TOK_TILE, D_MODEL), BF16),
                        pltpu.SemaphoreType.DMA((2,)),
                        pltpu.VMEM((SLOT_BLOCK, D_MODEL), F32)],
    )
    return pl.pallas_call(
        functools.partial(_gather_kernel, n_blocks=n_blocks),
        grid_spec=grid_spec,
        out_shape=jax.ShapeDtypeStruct((N_EXPERTS, cap, D_MODEL), BF16),
        compiler_params=_cparams(("arbitrary", "arbitrary")),
        name="gather",
    )(off_flat, lo, hi, lp_exp, hn)


def _ffn_kernel(xs_ref, wg_ref, wu_ref, wd_ref, eo_ref):
    xs = xs_ref[0]
    g = jnp.dot(xs, wg_ref[0], preferred_element_type=F32)
    u = jnp.dot(xs, wu_ref[0], preferred_element_type=F32)
    hid = (g * jax.nn.sigmoid(g) * u).astype(BF16)
    eo_ref[0] = jnp.dot(hid, wd_ref[0], preferred_element_type=F32).astype(BF16)


def _ffn(xs, w):
    cap = xs.shape[1]
    wspec = lambda shape: pl.BlockSpec(shape, lambda e, j: (e, 0, 0))
    return pl.pallas_call(
        _ffn_kernel,
        grid=(N_EXPERTS, cap // FFN_TM),
        in_specs=[pl.BlockSpec((1, FFN_TM, D_MODEL), lambda e, j: (e, j, 0)),
                  wspec((1, D_MODEL, EXPERT_FF)), wspec((1, D_MODEL, EXPERT_FF)),
                  wspec((1, EXPERT_FF, D_MODEL))],
        out_specs=pl.BlockSpec((1, FFN_TM, D_MODEL), lambda e, j: (e, j, 0)),
        out_shape=jax.ShapeDtypeStruct((N_EXPERTS, cap, D_MODEL), BF16),
        compiler_params=_cparams(("arbitrary", "arbitrary")),
        name="ffn",
    )(xs, w["wg"], w["wu"], w["wd"])


def _combine_kernel(off_ref, x1_ref, aff_ref, lp_ref, eo_hbm, y_ref, buf, sem, *, cap, n_tiles):
    i = pl.program_id(0)

    def window_start(tile, e):
        off = off_ref[tile * N_EXPERTS + e]
        aligned = (off // BF16_SUBLANES) * BF16_SUBLANES
        return pl.multiple_of(jnp.minimum(aligned, cap - READ_WINDOW), BF16_SUBLANES)

    def window_copy(tile, e, slot):
        return pltpu.make_async_copy(
            eo_hbm.at[e, pl.ds(window_start(tile, e), READ_WINDOW), :],
            buf.at[slot, e, pl.ds(0, READ_WINDOW), :], sem.at[slot, e])

    @pl.when(i == 0)
    def _():
        buf[:, :, READ_WINDOW:, :] = jnp.zeros(
            (2, N_EXPERTS, 2 * TOK_TILE - READ_WINDOW, D_MODEL), BF16)
        for e in range(N_EXPERTS):
            window_copy(0, e, 0).start()

    slot = i % 2

    @pl.when(i + 1 < n_tiles)
    def _():
        for e in range(N_EXPERTS):
            window_copy(i + 1, e, 1 - slot).start()

    lp = lp_ref[...]
    aff = aff_ref[...]
    lane = lax.broadcasted_iota(jnp.int32, (TOK_TILE, 2 * TOK_TILE), 1).astype(F32)
    acc = x1_ref[...]
    for e in range(N_EXPERTS):
        window_copy(i, e, slot).wait()
        shift = (off_ref[i * N_EXPERTS + e] - window_start(i, e)).astype(F32)
        lpe = lp[:, e:e + 1]
        onehot = jnp.where((lane == lpe + shift) & (lpe >= 0.0), 1.0, 0.0).astype(BF16)
        rows = jnp.dot(onehot, buf[slot, e], preferred_element_type=F32)
        acc = acc + aff[:, e:e + 1] * rows
    y_ref[...] = acc


def _combine(off_flat, x1, aff, lp_tok, eo):
    n_tok = x1.shape[0]
    n_tiles = n_tok // TOK_TILE
    cap = eo.shape[1]
    grid_spec = pltpu.PrefetchScalarGridSpec(
        num_scalar_prefetch=1,
        grid=(n_tiles,),
        in_specs=[pl.BlockSpec((TOK_TILE, D_MODEL), lambda i, *_: (i, 0)),
                  pl.BlockSpec((TOK_TILE, LANES), lambda i, *_: (i, 0)),
                  pl.BlockSpec((TOK_TILE, LANES), lambda i, *_: (i, 0)),
                  pl.BlockSpec(memory_space=pl.ANY)],
        out_specs=pl.BlockSpec((TOK_TILE, D_MODEL), lambda i, *_: (i, 0)),
        scratch_shapes=[pltpu.VMEM((2, N_EXPERTS, 2 * TOK_TILE, D_MODEL), BF16),
                        pltpu.SemaphoreType.DMA((2, N_EXPERTS))],
    )
    return pl.pallas_call(
        functools.partial(_combine_kernel, cap=cap, n_tiles=n_tiles),
        grid_spec=grid_spec,
        out_shape=jax.ShapeDtypeStruct((n_tok, D_MODEL), F32),
        compiler_params=_cparams(("arbitrary",)),
        name="combine",
    )(off_flat, x1, aff, lp_tok, eo)


def _rot_cols(w):
    return jnp.concatenate([-w[..., HALF_ROPE:], w[..., :HALF_ROPE]], axis=-1)


def _swap_halves(g):
    return jnp.concatenate([g[HALF_ROPE:], g[:HALF_ROPE]])


def _prep_weights(norm1_g, w_in, v_norm_g, w_spatial, b_spatial, w_a_out, q_norm_g, w_uq,
                  kv_norm_g, w_ukv, q_head_g, k_head_g, w_b_out, w_out, norm2_g,
                  w_router, w_gate, w_up, w_down):
    off_cq = 2 * A_WIDTH
    off_kr = off_cq + Q_LORA_RANK + KV_LORA_RANK
    off_ga = off_kr + QK_ROPE_DIM
    kr = w_in[:, off_kr:off_ga]
    w_in_p = jnp.concatenate([w_in[:, :off_ga], _rot_cols(kr), w_in[:, off_ga:]], axis=1)
    uq = w_uq.reshape(Q_LORA_RANK, N_HEADS, QK_HEAD_DIM)
    uq_rope = uq[..., QK_NOPE_DIM:]
    wuq = jnp.concatenate([uq, _rot_cols(uq_rope)], axis=-1).reshape(Q_LORA_RANK, N_HEADS * QK_PAD)
    ukv = w_ukv.reshape(KV_LORA_RANK, N_HEADS, QK_NOPE_DIM + V_HEAD_DIM)
    wukv = jnp.concatenate([ukv[..., :QK_NOPE_DIM].reshape(KV_LORA_RANK, -1),
                            ukv[..., QK_NOPE_DIM:].reshape(KV_LORA_RANK, -1)], axis=1)
    qg_r = q_head_g[QK_NOPE_DIM:]
    kg_r = k_head_g[QK_NOPE_DIM:]
    return {
        "g1": norm1_g.reshape(1, -1),
        "w_in": w_in_p.astype(BF16),
        "vg": v_norm_g.reshape(1, -1),
        "ws": w_spatial.astype(BF16),
        "bt": b_spatial.T,
        "wa": w_a_out.astype(BF16),
        "qng": q_norm_g.reshape(1, -1),
        "wuq": wuq.astype(BF16),
        "kvg": kv_norm_g.reshape(1, -1),
        "wukv": wukv.astype(BF16),
        "qg": jnp.concatenate([q_head_g, _swap_halves(qg_r)]).reshape(1, -1),
        "kgn": k_head_g[:QK_NOPE_DIM].reshape(1, -1),
        "kg2": jnp.concatenate([kg_r, _swap_halves(kg_r)]).reshape(1, -1),
        "wb": w_b_out.astype(BF16),
        "wo": w_out.astype(BF16),
        "g2": norm2_g.reshape(1, -1),
        "wr": jnp.pad(w_router, ((0, 0), (0, LANES - N_EXPERTS))).astype(BF16),
        "wg": w_gate.astype(BF16),
        "wu": w_up.astype(BF16),
        "wd": w_down.astype(BF16),
    }


def _rope_table(seq):
    inv_freq = ROPE_THETA ** (-jnp.arange(HALF_ROPE, dtype=F32) / HALF_ROPE)
    freqs = jnp.arange(seq, dtype=F32)[:, None] * inv_freq[None, :]
    cos, sin = jnp.cos(freqs), jnp.sin(freqs)
    return jnp.concatenate([cos, cos, sin, sin], axis=1)


def _block_tile_ranges(off, cap):
    n_blocks = cap // SLOT_BLOCK
    off_end = jnp.concatenate([off[1:], jnp.full((1, N_EXPERTS), cap, jnp.int32)], axis=0)
    block_start = jnp.arange(n_blocks, dtype=jnp.int32) * SLOT_BLOCK
    lo = jnp.sum(off_end[:, :, None] <= block_start[None, None, :], axis=0)
    hi = jnp.sum(off[:, :, None] < (block_start + SLOT_BLOCK)[None, None, :], axis=0) - 1
    return lo.astype(jnp.int32).reshape(-1), hi.astype(jnp.int32).reshape(-1)


def _layer_group(x, w):
    batch, seq, _ = x.shape
    n_tok = batch * seq
    cap = max(1, min(n_tok, CAPACITY_FACTOR * n_tok // N_EXPERTS))
    x2d = x.reshape(n_tok, D_MODEL)
    ab, gb, q, k, v = _pre(x2d, _rope_table(seq), w, batch, seq)
    o = _attn(q, k, v, batch, seq)
    x1, hn, aff, aff16 = _post(x2d, o.reshape(n_tok, D_MODEL), ab, gb, w)
    thr, need = _thresh(aff16, cap)
    lp_tok, lp_exp, off = _slots(aff, thr, need)
    off = off[:, 0, :N_EXPERTS].astype(jnp.int32)
    lo, hi = _block_tile_ranges(off, cap)
    off_flat = off.reshape(-1)
    xs = _gather(off_flat, lo, hi, lp_exp, hn, cap)
    eo = _ffn(xs, w)
    y = _combine(off_flat, x1, aff, lp_tok, eo)
    return y.reshape(batch, seq, D_MODEL)


def kernel(x_prompt, x_sample, norm1_g, w_in, v_norm_g, w_spatial, b_spatial, w_a_out, q_norm_g,
           w_uq, kv_norm_g, w_ukv, q_head_g, k_head_g, w_b_out, w_out, norm2_g, w_router,
           w_gate, w_up, w_down):
    params = (norm1_g, w_in, v_norm_g, w_spatial, b_spatial, w_a_out, q_norm_g, w_uq, kv_norm_g,
              w_ukv, q_head_g, k_head_g, w_b_out, w_out, norm2_g, w_router, w_gate, w_up, w_down)
    depth = norm1_g.shape[0]
    y_prompt, y_sample = x_prompt, x_sample
    for l in range(depth):
        w = _prep_weights(*[p[l] for p in params])
        y_prompt = _layer_group(y_prompt, w)
        y_sample = _layer_group(y_sample, w)
    return (y_prompt, y_sample)
```

```python
import functools
import math

import jax
import jax.numpy as jnp
from jax import lax
from jax.experimental import pallas as pl
from jax.experimental.pallas import tpu as pltpu

D_MODEL = 1024
CHUNK = 128
A_WIDTH = 1024
A_GROUPS = 8
A_GROUP_DIM = A_WIDTH // A_GROUPS
N_HEADS = 8
QK_NOPE_DIM = 128
QK_ROPE_DIM = 64
QK_HEAD_DIM = QK_NOPE_DIM + QK_ROPE_DIM
V_HEAD_DIM = 128
Q_LORA_RANK = 256
KV_LORA_RANK = 128
ROPE_THETA = 10000.0
N_EXPERTS = 16
EXPERT_FF = 1024
CAPACITY_FACTOR = 2
NORM_EPS = 1e-6

LANES = 128
BF16_SUBLANES = 16
VMEM_LIMIT_BYTES = 56 * 1024 * 1024

QK_PAD = 2 * LANES
HALF_ROPE = QK_ROPE_DIM // 2

P_UV = 0
P_SMALL = P_UV + 2 * A_WIDTH
P_SMALL_W = Q_LORA_RANK + KV_LORA_RANK + 2 * QK_ROPE_DIM
P_GA = P_SMALL + P_SMALL_W
P_GB = P_GA + D_MODEL
P_COLS = P_GB + D_MODEL

PRE_TM = 256
POST_TM = 512
ATT_TQ = 512
ATT_TK = 512
ATT_UNROLL = 4
TOK_TILE = 128
SLOT_BLOCK = 256
GATHER_WINDOW = 4
GATHER_PAD_ROWS = (GATHER_WINDOW - 1) * TOK_TILE
FFN_TM = 512
READ_WINDOW = TOK_TILE + BF16_SUBLANES

BF16 = jnp.bfloat16
F32 = jnp.float32


def _cparams(sem):
    return pltpu.CompilerParams(dimension_semantics=sem, vmem_limit_bytes=VMEM_LIMIT_BYTES)


def _rms(x, n):
    return lax.rsqrt(jnp.sum(x * x, axis=-1, keepdims=True) * (1.0 / n) + NORM_EPS)


def _const_spec(shape):
    nd = len(shape)
    return pl.BlockSpec(shape, lambda *_: (0,) * nd)


def _pre_kernel(x_ref, cs_ref, g1_ref, win_ref, vg_ref, ws_ref, bt_ref, wa_ref,
                qng_ref, wuq_ref, kvg_ref, wukv_ref, qg_ref, kgn_ref, kg2_ref,
                ab_ref, gb_ref, q_ref, k_ref, v_ref, sv_ref):
    x = x_ref[...]
    h = (x * _rms(x, D_MODEL) * g1_ref[...]).astype(BF16)

    uv = jnp.dot(h, win_ref[:, P_UV:P_SMALL], preferred_element_type=F32)
    uv = 0.5 * uv * (1.0 + lax.erf(uv * (1.0 / math.sqrt(2.0))))
    u = uv[:, :A_WIDTH]
    v = uv[:, A_WIDTH:]
    vn = (v * _rms(v, A_WIDTH) * vg_ref[...]).astype(BF16)
    for n in range(PRE_TM // CHUNK):
        for g in range(A_GROUPS):
            blk = vn[n * CHUNK:(n + 1) * CHUNK, g * A_GROUP_DIM:(g + 1) * A_GROUP_DIM]
            sv = jnp.dot(ws_ref[g], blk, preferred_element_type=F32) + bt_ref[:, g:g + 1]
            sv_ref[n * CHUNK:(n + 1) * CHUNK, g * A_GROUP_DIM:(g + 1) * A_GROUP_DIM] = sv
    a = jnp.dot((u * sv_ref[...]).astype(BF16), wa_ref[...], preferred_element_type=F32)
    ga = jax.nn.sigmoid(jnp.dot(h, win_ref[:, P_GA:P_GB], preferred_element_type=F32))
    ab_ref[...] = ga * a
    gb_ref[...] = jax.nn.sigmoid(jnp.dot(h, win_ref[:, P_GB:P_COLS], preferred_element_type=F32))

    small = jnp.dot(h, win_ref[:, P_SMALL:P_GA], preferred_element_type=F32)
    cq = small[:, :Q_LORA_RANK]
    ckv = small[:, Q_LORA_RANK:Q_LORA_RANK + KV_LORA_RANK]
    kr = small[:, Q_LORA_RANK + KV_LORA_RANK:]
    cs = cs_ref[...]
    lane = lax.broadcasted_iota(jnp.int32, (PRE_TM, LANES), 1)
    rope_lanes = lane < QK_ROPE_DIM

    cqn = (cq * _rms(cq, Q_LORA_RANK) * qng_ref[...]).astype(BF16)
    qf = jnp.dot(cqn, wuq_ref[...], preferred_element_type=F32)
    ckvn = (ckv * _rms(ckv, KV_LORA_RANK) * kvg_ref[...]).astype(BF16)
    kvf = jnp.dot(ckvn, wukv_ref[...], preferred_element_type=F32)

    kr_ss = jnp.sum(jnp.where(rope_lanes, kr * kr, 0.0), axis=-1, keepdims=True)
    t = kr * kg2_ref[...] * cs
    kro = jnp.where(rope_lanes, t + pltpu.roll(t, QK_ROPE_DIM, 1), 0.0)

    qg = qg_ref[...]
    scale = QK_HEAD_DIM ** -0.5 * math.log2(math.e)
    for hd in range(N_HEADS):
        qn = qf[:, hd * QK_PAD:hd * QK_PAD + LANES]
        q2 = qf[:, hd * QK_PAD + LANES:(hd + 1) * QK_PAD]
        q_ss = (jnp.sum(qn * qn, axis=-1, keepdims=True)
                + jnp.sum(jnp.where(rope_lanes, q2 * q2, 0.0), axis=-1, keepdims=True))
        q_rs = lax.rsqrt(q_ss * (1.0 / QK_HEAD_DIM) + NORM_EPS) * scale
        t = q2 * qg[:, LANES:] * cs
        q_ref[0, hd, :, :LANES] = (qn * qg[:, :LANES] * q_rs).astype(BF16)
        q_ref[0, hd, :, LANES:] = ((t + pltpu.roll(t, QK_ROPE_DIM, 1)) * q_rs).astype(BF16)

        kn = kvf[:, hd * QK_NOPE_DIM:(hd + 1) * QK_NOPE_DIM]
        k_ss = jnp.sum(kn * kn, axis=-1, keepdims=True) + kr_ss
        k_rs = lax.rsqrt(k_ss * (1.0 / QK_HEAD_DIM) + NORM_EPS)
        k_ref[0, hd, :, :LANES] = (kn * kgn_ref[...] * k_rs).astype(BF16)
        k_ref[0, hd, :, LANES:] = (kro * k_rs).astype(BF16)
        v_ref[0, hd] = kvf[:, N_HEADS * QK_NOPE_DIM + hd * V_HEAD_DIM:
                           N_HEADS * QK_NOPE_DIM + (hd + 1) * V_HEAD_DIM].astype(BF16)


def _pre(x2d, cs, w, batch, seq):
    n_tok = batch * seq
    tiles_per_seq = seq // PRE_TM
    tok = lambda i: (i, 0)
    head = lambda i: (i // tiles_per_seq, 0, i % tiles_per_seq, 0)
    in_specs = [
        pl.BlockSpec((PRE_TM, D_MODEL), tok),
        pl.BlockSpec((PRE_TM, LANES), lambda i: (i % tiles_per_seq, 0)),
        _const_spec((1, D_MODEL)),
        _const_spec((D_MODEL, P_COLS)),
        _const_spec((1, A_WIDTH)),
        _const_spec((A_GROUPS, CHUNK, CHUNK)),
        _const_spec((CHUNK, A_GROUPS)),
        _const_spec((A_WIDTH, D_MODEL)),
        _const_spec((1, Q_LORA_RANK)),
        _const_spec((Q_LORA_RANK, N_HEADS * QK_PAD)),
        _const_spec((1, KV_LORA_RANK)),
        _const_spec((KV_LORA_RANK, N_HEADS * (QK_NOPE_DIM + V_HEAD_DIM))),
        _const_spec((1, QK_PAD)),
        _const_spec((1, LANES)),
        _const_spec((1, LANES)),
    ]
    out_specs = [
        pl.BlockSpec((PRE_TM, D_MODEL), tok),
        pl.BlockSpec((PRE_TM, D_MODEL), tok),
        pl.BlockSpec((1, N_HEADS, PRE_TM, QK_PAD), head),
        pl.BlockSpec((1, N_HEADS, PRE_TM, QK_PAD), head),
        pl.BlockSpec((1, N_HEADS, PRE_TM, V_HEAD_DIM), head),
    ]
    out_shape = [
        jax.ShapeDtypeStruct((n_tok, D_MODEL), F32),
        jax.ShapeDtypeStruct((n_tok, D_MODEL), F32),
        jax.ShapeDtypeStruct((batch, N_HEADS, seq, QK_PAD), BF16),
        jax.ShapeDtypeStruct((batch, N_HEADS, seq, QK_PAD), BF16),
        jax.ShapeDtypeStruct((batch, N_HEADS, seq, V_HEAD_DIM), BF16),
    ]
    return pl.pallas_call(
        _pre_kernel,
        grid=(n_tok // PRE_TM,),
        in_specs=in_specs,
        out_specs=out_specs,
        out_shape=out_shape,
        scratch_shapes=[pltpu.VMEM((PRE_TM, A_WIDTH), F32)],
        compiler_params=_cparams(("arbitrary",)),
        name="pre",
    )(x2d, cs, w["g1"], w["w_in"], w["vg"], w["ws"], w["bt"], w["wa"], w["qng"], w["wuq"],
      w["kvg"], w["wukv"], w["qg"], w["kgn"], w["kg2"])


def _attn_kernel(q_ref, k_ref, v_ref, o_ref, m_ref, l_ref, acc_ref, *, seq):
    q = q_ref[0, 0]
    m_ref[...] = jnp.full(m_ref.shape, -jnp.inf, F32)
    l_ref[...] = jnp.zeros(l_ref.shape, F32)
    acc_ref[...] = jnp.zeros(acc_ref.shape, F32)
    n_chunks = ATT_TK // LANES

    def body(j, carry):
        start = pl.multiple_of(j * ATT_TK, ATT_TK)
        kt = k_ref[0, 0, pl.ds(start, ATT_TK), :]
        vt = v_ref[0, 0, pl.ds(start, ATT_TK), :]
        s = lax.dot_general(q, kt, (((1,), (1,)), ((), ())), preferred_element_type=F32)
        chunks = [s[:, c * LANES:(c + 1) * LANES] for c in range(n_chunks)]
        m_old = m_ref[...]
        row_max = jnp.max(functools.reduce(jnp.maximum, chunks), axis=-1, keepdims=True)
        m_new = jnp.maximum(m_old, row_max)
        alpha = jnp.exp2(m_old - m_new)
        ps = [jnp.exp2(c - m_new) for c in chunks]
        l_ref[...] = alpha * l_ref[...] + functools.reduce(jnp.add, ps)
        p = jnp.concatenate([x.astype(BF16) for x in ps], axis=1)
        acc_ref[...] = alpha * acc_ref[...] + jnp.dot(p, vt, preferred_element_type=F32)
        m_ref[...] = m_new
        return carry

    lax.fori_loop(0, seq // ATT_TK, body, 0, unroll=ATT_UNROLL)
    l = jnp.sum(l_ref[...], axis=-1, keepdims=True)
    o_ref[0] = (acc_ref[...] / l).astype(BF16)


def _attn(q, k, v, batch, seq):
    return pl.pallas_call(
        functools.partial(_attn_kernel, seq=seq),
        grid=(batch, N_HEADS, seq // ATT_TQ),
        in_specs=[
            pl.BlockSpec((1, 1, ATT_TQ, QK_PAD), lambda b, h, i: (b, h, i, 0)),
            pl.BlockSpec((1, 1, seq, QK_PAD), lambda b, h, i: (b, h, 0, 0)),
            pl.BlockSpec((1, 1, seq, V_HEAD_DIM), lambda b, h, i: (b, h, 0, 0)),
        ],
        out_specs=pl.BlockSpec((1, ATT_TQ, V_HEAD_DIM), lambda b, h, i: (b, i, h)),
        out_shape=jax.ShapeDtypeStruct((batch, seq, N_HEADS * V_HEAD_DIM), BF16),
        scratch_shapes=[pltpu.VMEM((ATT_TQ, LANES), F32), pltpu.VMEM((ATT_TQ, LANES), F32),
                        pltpu.VMEM((ATT_TQ, V_HEAD_DIM), F32)],
        compiler_params=_cparams(("arbitrary", "arbitrary", "arbitrary")),
        name="attn",
    )(q, k, v)


def _post_kernel(x_ref, o_ref, ab_ref, gb_ref, wb_ref, wo_ref, g2_ref, wr_ref,
                 x1_ref, hn_ref, aff_ref, aff16_ref):
    b = jnp.dot(o_ref[...], wb_ref[...], preferred_element_type=F32)
    mix = ab_ref[...] + gb_ref[...] * b
    x1 = x_ref[...] + jnp.dot(mix.astype(BF16), wo_ref[...], preferred_element_type=F32)
    x1_ref[...] = x1
    hn = (x1 * _rms(x1, D_MODEL) * g2_ref[...]).astype(BF16)
    hn_ref[...] = hn
    logits = jnp.dot(hn, wr_ref[...], preferred_element_type=F32)
    lane = lax.broadcasted_iota(jnp.int32, logits.shape, 1)
    logits = jnp.where(lane < N_EXPERTS, logits, -jnp.inf)
    e = jnp.exp(logits - jnp.max(logits, axis=-1, keepdims=True))
    aff = e / jnp.sum(e, axis=-1, keepdims=True)
    aff_ref[...] = aff
    aff16_ref[...] = aff[:, :N_EXPERTS]


def _post(x2d, o2d, ab, gb, w):
    n_tok = x2d.shape[0]
    tok = lambda i: (i, 0)
    return pl.pallas_call(
        _post_kernel,
        grid=(n_tok // POST_TM,),
        in_specs=[
            pl.BlockSpec((POST_TM, D_MODEL), tok),
            pl.BlockSpec((POST_TM, D_MODEL), tok),
            pl.BlockSpec((POST_TM, D_MODEL), tok),
            pl.BlockSpec((POST_TM, D_MODEL), tok),
            _const_spec((N_HEADS * V_HEAD_DIM, D_MODEL)),
            _const_spec((D_MODEL, D_MODEL)),
            _const_spec((1, D_MODEL)),
            _const_spec((D_MODEL, LANES)),
        ],
        out_specs=[
            pl.BlockSpec((POST_TM, D_MODEL), tok),
            pl.BlockSpec((POST_TM, D_MODEL), tok),
            pl.BlockSpec((POST_TM, LANES), tok),
            pl.BlockSpec((POST_TM, N_EXPERTS), tok),
        ],
        out_shape=[
            jax.ShapeDtypeStruct((n_tok, D_MODEL), F32),
            jax.ShapeDtypeStruct((n_tok, D_MODEL), BF16),
            jax.ShapeDtypeStruct((n_tok, LANES), F32),
            jax.ShapeDtypeStruct((n_tok, N_EXPERTS), F32),
        ],
        compiler_params=_cparams(("arbitrary",)),
        name="post",
    )(x2d, o2d, ab, gb, w["wb"], w["wo"], w["g2"], w["wr"])


def _thresh_kernel(aff_ref, thr_ref, need_ref, *, cap):
    bits = pltpu.bitcast(aff_ref[...], jnp.int32)

    def per_expert(c):
        for shift in (64, 32, 16):
            c = c + pltpu.roll(c, shift, 1)
        return c

    def body(i, cur):
        cand = cur | jnp.left_shift(jnp.int32(1), 30 - i)
        cnt = per_expert(jnp.sum(jnp.where(bits >= cand, 1.0, 0.0), axis=0, keepdims=True))
        return jnp.where(cnt >= cap, cand, cur)

    cur = lax.fori_loop(0, 31, body, jnp.zeros((1, LANES), jnp.int32))
    n_gt = per_expert(jnp.sum(jnp.where(bits > cur, 1.0, 0.0), axis=0, keepdims=True))
    thr_ref[...] = pltpu.bitcast(cur, F32)
    need_ref[...] = cap - n_gt


def _thresh(aff16, cap):
    n_tok = aff16.shape[0]
    packed = aff16.reshape(n_tok * N_EXPERTS // LANES, LANES)
    thr, need = pl.pallas_call(
        functools.partial(_thresh_kernel, cap=float(cap)),
        out_shape=[jax.ShapeDtypeStruct((1, LANES), F32), jax.ShapeDtypeStruct((1, LANES), F32)],
        compiler_params=pltpu.CompilerParams(vmem_limit_bytes=VMEM_LIMIT_BYTES),
        name="thresh",
    )(packed)
    lane = jnp.arange(LANES)
    thr = jnp.where(lane < N_EXPERTS, thr, jnp.inf)
    need = jnp.where(lane < N_EXPERTS, need, 0.0)
    return thr, need


def _slots_kernel(aff_ref, thr_ref, need_ref, lp_tok_ref, lp_exp_ref, off_ref, run_sel, run_eq):
    @pl.when(pl.program_id(0) == 0)
    def _():
        run_sel[...] = jnp.zeros(run_sel.shape, F32)
        run_eq[...] = jnp.zeros(run_eq.shape, F32)

    aff = aff_ref[...]
    thr = thr_ref[...]
    row = lax.broadcasted_iota(jnp.int32, (TOK_TILE, TOK_TILE), 0)
    col = lax.broadcasted_iota(jnp.int32, (TOK_TILE, TOK_TILE), 1)
    earlier = jnp.where(col < row, 1.0, 0.0).astype(BF16)
    eq = aff == thr
    eqf = jnp.where(eq, 1.0, 0.0)
    eq_rank = jnp.dot(earlier, eqf.astype(BF16), preferred_element_type=F32) + run_eq[...]
    sel = (aff > thr) | (eq & (eq_rank < need_ref[...]))
    self_ = jnp.where(sel, 1.0, 0.0)
    lpos = jnp.dot(earlier, self_.astype(BF16), preferred_element_type=F32)
    lp = jnp.where(sel, lpos, -1.0)
    lp_tok_ref[...] = lp
    lp_exp_ref[0] = lp.T[:N_EXPERTS, :]
    off_ref[0] = run_sel[...]
    run_sel[...] = run_sel[...] + jnp.sum(self_, axis=0, keepdims=True)
    run_eq[...] = run_eq[...] + jnp.sum(eqf, axis=0, keepdims=True)


def _slots(aff, thr, need):
    n_tok = aff.shape[0]
    n_tiles = n_tok // TOK_TILE
    return pl.pallas_call(
        _slots_kernel,
        grid=(n_tiles,),
        in_specs=[pl.BlockSpec((TOK_TILE, LANES), lambda i: (i, 0)),
                  _const_spec((1, LANES)), _const_spec((1, LANES))],
        out_specs=[pl.BlockSpec((TOK_TILE, LANES), lambda i: (i, 0)),
                   pl.BlockSpec((1, N_EXPERTS, TOK_TILE), lambda i: (i, 0, 0)),
                   pl.BlockSpec((1, 1, LANES), lambda i: (i, 0, 0))],
        out_shape=[jax.ShapeDtypeStruct((n_tok, LANES), F32),
                   jax.ShapeDtypeStruct((n_tiles, N_EXPERTS, TOK_TILE), F32),
                   jax.ShapeDtypeStruct((n_tiles, 1, LANES), F32)],
        scratch_shapes=[pltpu.VMEM((1, LANES), F32), pltpu.VMEM((1, LANES), F32)],
        compiler_params=_cparams(("arbitrary",)),
        name="slots",
    )(aff, thr, need)


def _gather_kernel(off_ref, lo_ref, hi_ref, lp_ref, hn_hbm, xs_ref, hn_vmem, sem, acc_ref, *,
                   n_blocks, n_tok):
    e = pl.program_id(0)
    j = pl.program_id(1)

    @pl.when((e == 0) & (j == 0))
    def _():
        hn_vmem[pl.ds(n_tok, GATHER_PAD_ROWS), :] = jnp.zeros((GATHER_PAD_ROWS, D_MODEL), BF16)
        load = pltpu.make_async_copy(hn_hbm, hn_vmem.at[pl.ds(0, n_tok), :], sem.at[0])
        load.start()
        load.wait()

    lo = lo_ref[e * n_blocks + j]
    hi = hi_ref[e * n_blocks + j]
    acc_ref[...] = jnp.zeros(acc_ref.shape, F32)
    slot_id = lax.broadcasted_iota(jnp.int32, (SLOT_BLOCK, TOK_TILE), 0).astype(F32)
    base = (j * SLOT_BLOCK).astype(F32)

    def body(i, carry):
        first = lo + i * GATHER_WINDOW
        pieces = []
        for t in range(GATHER_WINDOW):
            c = first + t
            lp = lp_ref[c, pl.ds(e, 1), :]
            dst = lp + (off_ref[c * N_EXPERTS + e].astype(F32) - base)
            pieces.append(jnp.where((slot_id == dst) & (lp >= 0.0), 1.0, 0.0).astype(BF16))
        onehot = jnp.concatenate(pieces, axis=1)
        start = pl.multiple_of(first * TOK_TILE, TOK_TILE)
        rows = hn_vmem[pl.ds(start, GATHER_WINDOW * TOK_TILE), :]
        acc_ref[...] += jnp.dot(onehot, rows, preferred_element_type=F32)
        return carry

    lax.fori_loop(0, (hi - lo + GATHER_WINDOW) // GATHER_WINDOW, body, 0)
    xs_ref[0] = acc_ref[...].astype(BF16)


def _gather(off, lo, hi, lp_exp, hn, cap):
    n_tok = hn.shape[0]
    n_tiles = lp_exp.shape[0]
    n_blocks = cap // SLOT_BLOCK
    pad = GATHER_WINDOW - 1
    lp_pad = jnp.pad(lp_exp, ((0, pad), (0, 0), (0, 0)), constant_values=-1.0)
    off_pad = jnp.pad(off, ((0, pad), (0, 0))).reshape(-1)
    grid_spec = pltpu.PrefetchScalarGridSpec(
        num_scalar_prefetch=3,
        grid=(N_EXPERTS, n_blocks),
        in_specs=[pl.BlockSpec((n_tiles + pad, N_EXPERTS, TOK_TILE), lambda e, j, *_: (0, 0, 0)),
                  pl.BlockSpec(memory_space=pl.ANY)],
        out_specs=pl.BlockSpec((1, SLOT_BLOCK, D_MODEL), lambda e, j, *_: (e, j, 0)),
        scratch_shapes=[pltpu.VMEM((n_tok + GATHER_PAD_ROWS, D_MODEL), BF16),
                        pltpu.SemaphoreType.DMA((1,)),
                        pltpu.VMEM((SLOT_BLOCK, D_MODEL), F32)],
    )
    return pl.pallas_call(
        functools.partial(_gather_kernel, n_blocks=n_blocks, n_tok=n_tok),
        grid_spec=grid_spec,
        out_shape=jax.ShapeDtypeStruct((N_EXPERTS, cap, D_MODEL), BF16),
        compiler_params=_cparams(("arbitrary", "arbitrary")),
        name="gather",
    )(off_pad, lo, hi, lp_pad, hn)


def _ffn_kernel(xs_ref, wg_ref, wu_ref, wd_ref, eo_ref):
    xs = xs_ref[0]
    g = jnp.dot(xs, wg_ref[0], preferred_element_type=F32)
    u = jnp.dot(xs, wu_ref[0], preferred_element_type=F32)
    hid = (g * jax.nn.sigmoid(g) * u).astype(BF16)
    eo_ref[0] = jnp.dot(hid, wd_ref[0], preferred_element_type=F32).astype(BF16)


def _ffn(xs, w):
    cap = xs.shape[1]
    wspec = lambda shape: pl.BlockSpec(shape, lambda e, j: (e, 0, 0))
    return pl.pallas_call(
        _ffn_kernel,
        grid=(N_EXPERTS, cap // FFN_TM),
        in_specs=[pl.BlockSpec((1, FFN_TM, D_MODEL), lambda e, j: (e, j, 0)),
                  wspec((1, D_MODEL, EXPERT_FF)), wspec((1, D_MODEL, EXPERT_FF)),
                  wspec((1, EXPERT_FF, D_MODEL))],
        out_specs=pl.BlockSpec((1, FFN_TM, D_MODEL), lambda e, j: (e, j, 0)),
        out_shape=jax.ShapeDtypeStruct((N_EXPERTS, cap, D_MODEL), BF16),
        compiler_params=_cparams(("arbitrary", "arbitrary")),
        name="ffn",
    )(xs, w["wg"], w["wu"], w["wd"])


def _combine_kernel(off_ref, x1_ref, aff_ref, lp_ref, eo_hbm, y_ref, buf, sem, *, cap, n_tiles):
    i = pl.program_id(0)

    def window_start(tile, e):
        off = off_ref[tile * N_EXPERTS + e]
        aligned = (off // BF16_SUBLANES) * BF16_SUBLANES
        return pl.multiple_of(jnp.minimum(aligned, cap - READ_WINDOW), BF16_SUBLANES)

    def window_copy(tile, e, slot):
        return pltpu.make_async_copy(
            eo_hbm.at[e, pl.ds(window_start(tile, e), READ_WINDOW), :],
            buf.at[slot, e, pl.ds(0, READ_WINDOW), :], sem.at[slot, e])

    @pl.when(i == 0)
    def _():
        buf[:, :, READ_WINDOW:, :] = jnp.zeros(
            (2, N_EXPERTS, 2 * TOK_TILE - READ_WINDOW, D_MODEL), BF16)
        for e in range(N_EXPERTS):
            window_copy(0, e, 0).start()

    slot = i % 2

    @pl.when(i + 1 < n_tiles)
    def _():
        for e in range(N_EXPERTS):
            window_copy(i + 1, e, 1 - slot).start()

    lp = lp_ref[...]
    aff = aff_ref[...]
    lane = lax.broadcasted_iota(jnp.int32, (TOK_TILE, 2 * TOK_TILE), 1).astype(F32)
    acc = x1_ref[...]
    for e in range(N_EXPERTS):
        window_copy(i, e, slot).wait()
        shift = (off_ref[i * N_EXPERTS + e] - window_start(i, e)).astype(F32)
        lpe = lp[:, e:e + 1]
        onehot = jnp.where((lane == lpe + shift) & (lpe >= 0.0), 1.0, 0.0).astype(BF16)
        rows = jnp.dot(onehot, buf[slot, e], preferred_element_type=F32)
        acc = acc + aff[:, e:e + 1] * rows
    y_ref[...] = acc


def _combine(off_flat, x1, aff, lp_tok, eo):
    n_tok = x1.shape[0]
    n_tiles = n_tok // TOK_TILE
    cap = eo.shape[1]
    grid_spec = pltpu.PrefetchScalarGridSpec(
        num_scalar_prefetch=1,
        grid=(n_tiles,),
        in_specs=[pl.BlockSpec((TOK_TILE, D_MODEL), lambda i, *_: (i, 0)),
                  pl.BlockSpec((TOK_TILE, LANES), lambda i, *_: (i, 0)),
                  pl.BlockSpec((TOK_TILE, LANES), lambda i, *_: (i, 0)),
                  pl.BlockSpec(memory_space=pl.ANY)],
        out_specs=pl.BlockSpec((TOK_TILE, D_MODEL), lambda i, *_: (i, 0)),
        scratch_shapes=[pltpu.VMEM((2, N_EXPERTS, 2 * TOK_TILE, D_MODEL), BF16),
                        pltpu.SemaphoreType.DMA((2, N_EXPERTS))],
    )
    return pl.pallas_call(
        functools.partial(_combine_kernel, cap=cap, n_tiles=n_tiles),
        grid_spec=grid_spec,
        out_shape=jax.ShapeDtypeStruct((n_tok, D_MODEL), F32),
        compiler_params=_cparams(("arbitrary",)),
        name="combine",
    )(off_flat, x1, aff, lp_tok, eo)


def _rot_cols(w):
    return jnp.concatenate([-w[..., HALF_ROPE:], w[..., :HALF_ROPE]], axis=-1)


def _swap_halves(g):
    return jnp.concatenate([g[HALF_ROPE:], g[:HALF_ROPE]])


def _prep_weights(norm1_g, w_in, v_norm_g, w_spatial, b_spatial, w_a_out, q_norm_g, w_uq,
                  kv_norm_g, w_ukv, q_head_g, k_head_g, w_b_out, w_out, norm2_g,
                  w_router, w_gate, w_up, w_down):
    off_cq = 2 * A_WIDTH
    off_kr = off_cq + Q_LORA_RANK + KV_LORA_RANK
    off_ga = off_kr + QK_ROPE_DIM
    kr = w_in[:, off_kr:off_ga]
    w_in_p = jnp.concatenate([w_in[:, :off_ga], _rot_cols(kr), w_in[:, off_ga:]], axis=1)
    uq = w_uq.reshape(Q_LORA_RANK, N_HEADS, QK_HEAD_DIM)
    uq_rope = uq[..., QK_NOPE_DIM:]
    wuq = jnp.concatenate([uq, _rot_cols(uq_rope)], axis=-1).reshape(Q_LORA_RANK, N_HEADS * QK_PAD)
    ukv = w_ukv.reshape(KV_LORA_RANK, N_HEADS, QK_NOPE_DIM + V_HEAD_DIM)
    wukv = jnp.concatenate([ukv[..., :QK_NOPE_DIM].reshape(KV_LORA_RANK, -1),
                            ukv[..., QK_NOPE_DIM:].reshape(KV_LORA_RANK, -1)], axis=1)
    qg_r = q_head_g[QK_NOPE_DIM:]
    kg_r = k_head_g[QK_NOPE_DIM:]
    return {
        "g1": norm1_g.reshape(1, -1),
        "w_in": w_in_p.astype(BF16),
        "vg": v_norm_g.reshape(1, -1),
        "ws": w_spatial.astype(BF16),
        "bt": b_spatial.T,
        "wa": w_a_out.astype(BF16),
        "qng": q_norm_g.reshape(1, -1),
        "wuq": wuq.astype(BF16),
        "kvg": kv_norm_g.reshape(1, -1),
        "wukv": wukv.astype(BF16),
        "qg": jnp.concatenate([q_head_g, _swap_halves(qg_r)]).reshape(1, -1),
        "kgn": k_head_g[:QK_NOPE_DIM].reshape(1, -1),
        "kg2": jnp.concatenate([kg_r, _swap_halves(kg_r)]).reshape(1, -1),
        "wb": w_b_out.astype(BF16),
        "wo": w_out.astype(BF16),
        "g2": norm2_g.reshape(1, -1),
        "wr": jnp.pad(w_router, ((0, 0), (0, LANES - N_EXPERTS))).astype(BF16),
        "wg": w_gate.astype(BF16),
        "wu": w_up.astype(BF16),
        "wd": w_down.astype(BF16),
    }


def _rope_table(seq):
    inv_freq = ROPE_THETA ** (-jnp.arange(HALF_ROPE, dtype=F32) / HALF_ROPE)
    freqs = jnp.arange(seq, dtype=F32)[:, None] * inv_freq[None, :]
    cos, sin = jnp.cos(freqs), jnp.sin(freqs)
    return jnp.concatenate([cos, cos, sin, sin], axis=1)


def _block_tile_ranges(off, cap):
    n_blocks = cap // SLOT_BLOCK
    off_end = jnp.concatenate([off[1:], jnp.full((1, N_EXPERTS), cap, jnp.int32)], axis=0)
    block_start = jnp.arange(n_blocks, dtype=jnp.int32) * SLOT_BLOCK
    lo = jnp.sum(off_end[:, :, None] <= block_start[None, None, :], axis=0)
    hi = jnp.sum(off[:, :, None] < (block_start + SLOT_BLOCK)[None, None, :], axis=0) - 1
    return lo.astype(jnp.int32).reshape(-1), hi.astype(jnp.int32).reshape(-1)


def _layer_group(x, w):
    batch, seq, _ = x.shape
    n_tok = batch * seq
    cap = max(1, min(n_tok, CAPACITY_FACTOR * n_tok // N_EXPERTS))
    x2d = x.reshape(n_tok, D_MODEL)
    ab, gb, q, k, v = _pre(x2d, _rope_table(seq), w, batch, seq)
    o = _attn(q, k, v, batch, seq)
    x1, hn, aff, aff16 = _post(x2d, o.reshape(n_tok, D_MODEL), ab, gb, w)
    thr, need = _thresh(aff16, cap)
    lp_tok, lp_exp, off = _slots(aff, thr, need)
    off = off[:, 0, :N_EXPERTS].astype(jnp.int32)
    lo, hi = _block_tile_ranges(off, cap)
    off_flat = off.reshape(-1)
    xs = _gather(off, lo, hi, lp_exp, hn, cap)
    eo = _ffn(xs, w)
    y = _combine(off_flat, x1, aff, lp_tok, eo)
    return y.reshape(batch, seq, D_MODEL)


def kernel(x_prompt, x_sample, norm1_g, w_in, v_norm_g, w_spatial, b_spatial, w_a_out, q_norm_g,
           w_uq, kv_norm_g, w_ukv, q_head_g, k_head_g, w_b_out, w_out, norm2_g, w_router,
           w_gate, w_up, w_down):
    params = (norm1_g, w_in, v_norm_g, w_spatial, b_spatial, w_a_out, q_norm_g, w_uq, kv_norm_g,
              w_ukv, q_head_g, k_head_g, w_b_out, w_out, norm2_g, w_router, w_gate, w_up, w_down)
    depth = norm1_g.shape[0]
    y_prompt, y_sample = x_prompt, x_sample
    for l in range(depth):
        w = _prep_weights(*[p[l] for p in params])
        y_prompt = _layer_group(y_prompt, w)
        y_sample = _layer_group(y_sample, w)
    return (y_prompt, y_sample)
```

```python
import functools
import math

import jax
import jax.numpy as jnp
from jax import lax
from jax.experimental import pallas as pl
from jax.experimental.pallas import tpu as pltpu

D_MODEL = 1024
CHUNK = 128
A_WIDTH = 1024
A_GROUPS = 8
A_GROUP_DIM = A_WIDTH // A_GROUPS
N_HEADS = 8
QK_NOPE_DIM = 128
QK_ROPE_DIM = 64
QK_HEAD_DIM = QK_NOPE_DIM + QK_ROPE_DIM
V_HEAD_DIM = 128
Q_LORA_RANK = 256
KV_LORA_RANK = 128
ROPE_THETA = 10000.0
N_EXPERTS = 16
EXPERT_FF = 1024
CAPACITY_FACTOR = 2
NORM_EPS = 1e-6

LANES = 128
BF16_SUBLANES = 16
VMEM_LIMIT_BYTES = 56 * 1024 * 1024

QK_PAD = 2 * LANES
HALF_ROPE = QK_ROPE_DIM // 2

P_UV = 0
P_SMALL = P_UV + 2 * A_WIDTH
P_SMALL_W = Q_LORA_RANK + KV_LORA_RANK + 2 * QK_ROPE_DIM
P_GA = P_SMALL + P_SMALL_W
P_GB = P_GA + D_MODEL
P_COLS = P_GB + D_MODEL

PRE_TM = 256
POST_TM = 512
ATT_TQ = 512
ATT_TK = 512
ATT_UNROLL = 4
TOK_TILE = 128
SLOTS_TILES = 4
SLOT_BLOCK = 256
GATHER_WINDOW = 4
GATHER_PAD_ROWS = (GATHER_WINDOW - 1) * TOK_TILE
FFN_TM = 512
READ_WINDOW = TOK_TILE + BF16_SUBLANES

BF16 = jnp.bfloat16
F32 = jnp.float32


def _cparams(sem):
    return pltpu.CompilerParams(dimension_semantics=sem, vmem_limit_bytes=VMEM_LIMIT_BYTES)


def _rms(x, n):
    return lax.rsqrt(jnp.sum(x * x, axis=-1, keepdims=True) * (1.0 / n) + NORM_EPS)


def _const_spec(shape):
    nd = len(shape)
    return pl.BlockSpec(shape, lambda *_: (0,) * nd)


def _pre_kernel(x_ref, cs_ref, g1_ref, win_ref, vg_ref, ws_ref, bt_ref, wa_ref,
                qng_ref, wuq_ref, kvg_ref, wukv_ref, qg_ref, kgn_ref, kg2_ref,
                ab_ref, gb_ref, q_ref, k_ref, v_ref, sv_ref):
    x = x_ref[...]
    h = (x * _rms(x, D_MODEL) * g1_ref[...]).astype(BF16)

    uv = jnp.dot(h, win_ref[:, P_UV:P_SMALL], preferred_element_type=F32)
    uv = 0.5 * uv * (1.0 + lax.erf(uv * (1.0 / math.sqrt(2.0))))
    u = uv[:, :A_WIDTH]
    v = uv[:, A_WIDTH:]
    vn = (v * _rms(v, A_WIDTH) * vg_ref[...]).astype(BF16)
    for n in range(PRE_TM // CHUNK):
        for g in range(A_GROUPS):
            blk = vn[n * CHUNK:(n + 1) * CHUNK, g * A_GROUP_DIM:(g + 1) * A_GROUP_DIM]
            sv = jnp.dot(ws_ref[g], blk, preferred_element_type=F32) + bt_ref[:, g:g + 1]
            sv_ref[n * CHUNK:(n + 1) * CHUNK, g * A_GROUP_DIM:(g + 1) * A_GROUP_DIM] = sv
    a = jnp.dot((u * sv_ref[...]).astype(BF16), wa_ref[...], preferred_element_type=F32)
    ga = jax.nn.sigmoid(jnp.dot(h, win_ref[:, P_GA:P_GB], preferred_element_type=F32))
    ab_ref[...] = ga * a
    gb_ref[...] = jax.nn.sigmoid(jnp.dot(h, win_ref[:, P_GB:P_COLS], preferred_element_type=F32))

    small = jnp.dot(h, win_ref[:, P_SMALL:P_GA], preferred_element_type=F32)
    cq = small[:, :Q_LORA_RANK]
    ckv = small[:, Q_LORA_RANK:Q_LORA_RANK + KV_LORA_RANK]
    kr = small[:, Q_LORA_RANK + KV_LORA_RANK:]
    cs = cs_ref[...]
    lane = lax.broadcasted_iota(jnp.int32, (PRE_TM, LANES), 1)
    rope_lanes = lane < QK_ROPE_DIM

    cqn = (cq * _rms(cq, Q_LORA_RANK) * qng_ref[...]).astype(BF16)
    qf = jnp.dot(cqn, wuq_ref[...], preferred_element_type=F32)
    ckvn = (ckv * _rms(ckv, KV_LORA_RANK) * kvg_ref[...]).astype(BF16)
    kvf = jnp.dot(ckvn, wukv_ref[...], preferred_element_type=F32)

    kr_ss = jnp.sum(jnp.where(rope_lanes, kr * kr, 0.0), axis=-1, keepdims=True)
    t = kr * kg2_ref[...] * cs
    kro = jnp.where(rope_lanes, t + pltpu.roll(t, QK_ROPE_DIM, 1), 0.0)

    qg = qg_ref[...]
    scale = QK_HEAD_DIM ** -0.5 * math.log2(math.e)
    for hd in range(N_HEADS):
        qn = qf[:, hd * QK_PAD:hd * QK_PAD + LANES]
        q2 = qf[:, hd * QK_PAD + LANES:(hd + 1) * QK_PAD]
        q_ss = (jnp.sum(qn * qn, axis=-1, keepdims=True)
                + jnp.sum(jnp.where(rope_lanes, q2 * q2, 0.0), axis=-1, keepdims=True))
        q_rs = lax.rsqrt(q_ss * (1.0 / QK_HEAD_DIM) + NORM_EPS) * scale
        t = q2 * qg[:, LANES:] * cs
        q_ref[0, hd, :, :LANES] = (qn * qg[:, :LANES] * q_rs).astype(BF16)
        q_ref[0, hd, :, LANES:] = ((t + pltpu.roll(t, QK_ROPE_DIM, 1)) * q_rs).astype(BF16)

        kn = kvf[:, hd * QK_NOPE_DIM:(hd + 1) * QK_NOPE_DIM]
        k_ss = jnp.sum(kn * kn, axis=-1, keepdims=True) + kr_ss
        k_rs = lax.rsqrt(k_ss * (1.0 / QK_HEAD_DIM) + NORM_EPS)
        k_ref[0, hd, :, :LANES] = (kn * kgn_ref[...] * k_rs).astype(BF16)
        k_ref[0, hd, :, LANES:] = (kro * k_rs).astype(BF16)
        v_ref[0, hd] = kvf[:, N_HEADS * QK_NOPE_DIM + hd * V_HEAD_DIM:
                           N_HEADS * QK_NOPE_DIM + (hd + 1) * V_HEAD_DIM].astype(BF16)


def _pre(x2d, cs, w, batch, seq):
    n_tok = batch * seq
    tiles_per_seq = seq // PRE_TM
    tok = lambda i: (i, 0)
    head = lambda i: (i // tiles_per_seq, 0, i % tiles_per_seq, 0)
    in_specs = [
        pl.BlockSpec((PRE_TM, D_MODEL), tok),
        pl.BlockSpec((PRE_TM, LANES), lambda i: (i % tiles_per_seq, 0)),
        _const_spec((1, D_MODEL)),
        _const_spec((D_MODEL, P_COLS)),
        _const_spec((1, A_WIDTH)),
        _const_spec((A_GROUPS, CHUNK, CHUNK)),
        _const_spec((CHUNK, A_GROUPS)),
        _const_spec((A_WIDTH, D_MODEL)),
        _const_spec((1, Q_LORA_RANK)),
        _const_spec((Q_LORA_RANK, N_HEADS * QK_PAD)),
        _const_spec((1, KV_LORA_RANK)),
        _const_spec((KV_LORA_RANK, N_HEADS * (QK_NOPE_DIM + V_HEAD_DIM))),
        _const_spec((1, QK_PAD)),
        _const_spec((1, LANES)),
        _const_spec((1, LANES)),
    ]
    out_specs = [
        pl.BlockSpec((PRE_TM, D_MODEL), tok),
        pl.BlockSpec((PRE_TM, D_MODEL), tok),
        pl.BlockSpec((1, N_HEADS, PRE_TM, QK_PAD), head),
        pl.BlockSpec((1, N_HEADS, PRE_TM, QK_PAD), head),
        pl.BlockSpec((1, N_HEADS, PRE_TM, V_HEAD_DIM), head),
    ]
    out_shape = [
        jax.ShapeDtypeStruct((n_tok, D_MODEL), F32),
        jax.ShapeDtypeStruct((n_tok, D_MODEL), F32),
        jax.ShapeDtypeStruct((batch, N_HEADS, seq, QK_PAD), BF16),
        jax.ShapeDtypeStruct((batch, N_HEADS, seq, QK_PAD), BF16),
        jax.ShapeDtypeStruct((batch, N_HEADS, seq, V_HEAD_DIM), BF16),
    ]
    return pl.pallas_call(
        _pre_kernel,
        grid=(n_tok // PRE_TM,),
        in_specs=in_specs,
        out_specs=out_specs,
        out_shape=out_shape,
        scratch_shapes=[pltpu.VMEM((PRE_TM, A_WIDTH), F32)],
        compiler_params=_cparams(("arbitrary",)),
        name="pre",
    )(x2d, cs, w["g1"], w["w_in"], w["vg"], w["ws"], w["bt"], w["wa"], w["qng"], w["wuq"],
      w["kvg"], w["wukv"], w["qg"], w["kgn"], w["kg2"])


def _attn_kernel(q_ref, k_ref, v_ref, o_ref, m_ref, l_ref, acc_ref, *, seq):
    q = q_ref[0, 0]
    m_ref[...] = jnp.full(m_ref.shape, -jnp.inf, F32)
    l_ref[...] = jnp.zeros(l_ref.shape, F32)
    acc_ref[...] = jnp.zeros(acc_ref.shape, F32)
    n_chunks = ATT_TK // LANES

    def body(j, carry):
        start = pl.multiple_of(j * ATT_TK, ATT_TK)
        kt = k_ref[0, 0, pl.ds(start, ATT_TK), :]
        vt = v_ref[0, 0, pl.ds(start, ATT_TK), :]
        s = lax.dot_general(q, kt, (((1,), (1,)), ((), ())), preferred_element_type=F32)
        chunks = [s[:, c * LANES:(c + 1) * LANES] for c in range(n_chunks)]
        m_old = m_ref[...]
        row_max = jnp.max(functools.reduce(jnp.maximum, chunks), axis=-1, keepdims=True)
        m_new = jnp.maximum(m_old, row_max)
        alpha = jnp.exp2(m_old - m_new)
        ps = [jnp.exp2(c - m_new) for c in chunks]
        l_ref[...] = alpha * l_ref[...] + functools.reduce(jnp.add, ps)
        p = jnp.concatenate([x.astype(BF16) for x in ps], axis=1)
        acc_ref[...] = alpha * acc_ref[...] + jnp.dot(p, vt, preferred_element_type=F32)
        m_ref[...] = m_new
        return carry

    lax.fori_loop(0, seq // ATT_TK, body, 0, unroll=ATT_UNROLL)
    l = jnp.sum(l_ref[...], axis=-1, keepdims=True)
    o_ref[0] = (acc_ref[...] / l).astype(BF16)


def _attn(q, k, v, batch, seq):
    return pl.pallas_call(
        functools.partial(_attn_kernel, seq=seq),
        grid=(batch, N_HEADS, seq // ATT_TQ),
        in_specs=[
            pl.BlockSpec((1, 1, ATT_TQ, QK_PAD), lambda b, h, i: (b, h, i, 0)),
            pl.BlockSpec((1, 1, seq, QK_PAD), lambda b, h, i: (b, h, 0, 0)),
            pl.BlockSpec((1, 1, seq, V_HEAD_DIM), lambda b, h, i: (b, h, 0, 0)),
        ],
        out_specs=pl.BlockSpec((1, ATT_TQ, V_HEAD_DIM), lambda b, h, i: (b, i, h)),
        out_shape=jax.ShapeDtypeStruct((batch, seq, N_HEADS * V_HEAD_DIM), BF16),
        scratch_shapes=[pltpu.VMEM((ATT_TQ, LANES), F32), pltpu.VMEM((ATT_TQ, LANES), F32),
                        pltpu.VMEM((ATT_TQ, V_HEAD_DIM), F32)],
        compiler_params=_cparams(("arbitrary", "arbitrary", "arbitrary")),
        name="attn",
    )(q, k, v)


def _post_kernel(x_ref, o_ref, ab_ref, gb_ref, wb_ref, wo_ref, g2_ref, wr_ref,
                 x1_ref, hn_ref, aff_ref, aff16_ref):
    b = jnp.dot(o_ref[...], wb_ref[...], preferred_element_type=F32)
    mix = ab_ref[...] + gb_ref[...] * b
    x1 = x_ref[...] + jnp.dot(mix.astype(BF16), wo_ref[...], preferred_element_type=F32)
    x1_ref[...] = x1
    hn = (x1 * _rms(x1, D_MODEL) * g2_ref[...]).astype(BF16)
    hn_ref[...] = hn
    logits = jnp.dot(hn, wr_ref[...], preferred_element_type=F32)
    lane = lax.broadcasted_iota(jnp.int32, logits.shape, 1)
    logits = jnp.where(lane < N_EXPERTS, logits, -jnp.inf)
    e = jnp.exp(logits - jnp.max(logits, axis=-1, keepdims=True))
    aff = e / jnp.sum(e, axis=-1, keepdims=True)
    aff_ref[...] = aff
    aff16_ref[...] = aff[:, :N_EXPERTS]


def _post(x2d, o2d, ab, gb, w):
    n_tok = x2d.shape[0]
    tok = lambda i: (i, 0)
    return pl.pallas_call(
        _post_kernel,
        grid=(n_tok // POST_TM,),
        in_specs=[
            pl.BlockSpec((POST_TM, D_MODEL), tok),
            pl.BlockSpec((POST_TM, D_MODEL), tok),
            pl.BlockSpec((POST_TM, D_MODEL), tok),
            pl.BlockSpec((POST_TM, D_MODEL), tok),
            _const_spec((N_HEADS * V_HEAD_DIM, D_MODEL)),
            _const_spec((D_MODEL, D_MODEL)),
            _const_spec((1, D_MODEL)),
            _const_spec((D_MODEL, LANES)),
        ],
        out_specs=[
            pl.BlockSpec((POST_TM, D_MODEL), tok),
            pl.BlockSpec((POST_TM, D_MODEL), tok),
            pl.BlockSpec((POST_TM, LANES), tok),
            pl.BlockSpec((POST_TM, N_EXPERTS), tok),
        ],
        out_shape=[
            jax.ShapeDtypeStruct((n_tok, D_MODEL), F32),
            jax.ShapeDtypeStruct((n_tok, D_MODEL), BF16),
            jax.ShapeDtypeStruct((n_tok, LANES), F32),
            jax.ShapeDtypeStruct((n_tok, N_EXPERTS), F32),
        ],
        compiler_params=_cparams(("arbitrary",)),
        name="post",
    )(x2d, o2d, ab, gb, w["wb"], w["wo"], w["g2"], w["wr"])


def _thresh_kernel(aff_ref, thr_ref, need_ref, *, cap):
    bits = pltpu.bitcast(aff_ref[...], jnp.int32)

    def per_expert(c):
        for shift in (64, 32, 16):
            c = c + pltpu.roll(c, shift, 1)
        return c

    def body(i, cur):
        cand = cur | jnp.left_shift(jnp.int32(1), 30 - i)
        cnt = per_expert(jnp.sum(jnp.where(bits >= cand, 1.0, 0.0), axis=0, keepdims=True))
        return jnp.where(cnt >= cap, cand, cur)

    cur = lax.fori_loop(0, 31, body, jnp.zeros((1, LANES), jnp.int32))
    n_gt = per_expert(jnp.sum(jnp.where(bits > cur, 1.0, 0.0), axis=0, keepdims=True))
    thr_ref[...] = pltpu.bitcast(cur, F32)
    need_ref[...] = cap - n_gt


def _thresh(aff16, cap):
    n_tok = aff16.shape[0]
    packed = aff16.reshape(n_tok * N_EXPERTS // LANES, LANES)
    thr, need = pl.pallas_call(
        functools.partial(_thresh_kernel, cap=float(cap)),
        out_shape=[jax.ShapeDtypeStruct((1, LANES), F32), jax.ShapeDtypeStruct((1, LANES), F32)],
        compiler_params=pltpu.CompilerParams(vmem_limit_bytes=VMEM_LIMIT_BYTES),
        name="thresh",
    )(packed)
    lane = jnp.arange(LANES)
    thr = jnp.where(lane < N_EXPERTS, thr, jnp.inf)
    need = jnp.where(lane < N_EXPERTS, need, 0.0)
    return thr, need


def _slots_kernel(aff_ref, thr_ref, need_ref, lp_tok_ref, lp_exp_ref, off_ref, run_sel, run_eq):
    @pl.when(pl.program_id(0) == 0)
    def _():
        run_sel[...] = jnp.zeros(run_sel.shape, F32)
        run_eq[...] = jnp.zeros(run_eq.shape, F32)

    thr = thr_ref[...]
    row = lax.broadcasted_iota(jnp.int32, (TOK_TILE, TOK_TILE), 0)
    col = lax.broadcasted_iota(jnp.int32, (TOK_TILE, TOK_TILE), 1)
    earlier = jnp.where(col < row, 1.0, 0.0).astype(BF16)
    for t in range(SLOTS_TILES):
        aff = aff_ref[t * TOK_TILE:(t + 1) * TOK_TILE, :]
        eq = aff == thr
        eqf = jnp.where(eq, 1.0, 0.0)
        eq_rank = jnp.dot(earlier, eqf.astype(BF16), preferred_element_type=F32) + run_eq[...]
        sel = (aff > thr) | (eq & (eq_rank < need_ref[...]))
        self_ = jnp.where(sel, 1.0, 0.0)
        lpos = jnp.dot(earlier, self_.astype(BF16), preferred_element_type=F32)
        lp = jnp.where(sel, lpos, -1.0)
        lp_tok_ref[t * TOK_TILE:(t + 1) * TOK_TILE, :] = lp
        lp_exp_ref[t] = lp.T[:N_EXPERTS, :]
        off_ref[t] = run_sel[...]
        run_sel[...] = run_sel[...] + jnp.sum(self_, axis=0, keepdims=True)
        run_eq[...] = run_eq[...] + jnp.sum(eqf, axis=0, keepdims=True)


def _slots(aff, thr, need):
    n_tok = aff.shape[0]
    n_tiles = n_tok // TOK_TILE
    return pl.pallas_call(
        _slots_kernel,
        grid=(n_tiles // SLOTS_TILES,),
        in_specs=[pl.BlockSpec((SLOTS_TILES * TOK_TILE, LANES), lambda i: (i, 0)),
                  _const_spec((1, LANES)), _const_spec((1, LANES))],
        out_specs=[pl.BlockSpec((SLOTS_TILES * TOK_TILE, LANES), lambda i: (i, 0)),
                   pl.BlockSpec((SLOTS_TILES, N_EXPERTS, TOK_TILE), lambda i: (i, 0, 0)),
                   pl.BlockSpec((SLOTS_TILES, 1, LANES), lambda i: (i, 0, 0))],
        out_shape=[jax.ShapeDtypeStruct((n_tok, LANES), F32),
                   jax.ShapeDtypeStruct((n_tiles, N_EXPERTS, TOK_TILE), F32),
                   jax.ShapeDtypeStruct((n_tiles, 1, LANES), F32)],
        scratch_shapes=[pltpu.VMEM((1, LANES), F32), pltpu.VMEM((1, LANES), F32)],
        compiler_params=_cparams(("arbitrary",)),
        name="slots",
    )(aff, thr, need)


def _gather_kernel(off_ref, lo_ref, hi_ref, lp_ref, hn_hbm, xs_ref, hn_vmem, sem, acc_ref, *,
                   n_blocks, n_tok):
    e = pl.program_id(0)
    j = pl.program_id(1)

    @pl.when((e == 0) & (j == 0))
    def _():
        hn_vmem[pl.ds(n_tok, GATHER_PAD_ROWS), :] = jnp.zeros((GATHER_PAD_ROWS, D_MODEL), BF16)
        load = pltpu.make_async_copy(hn_hbm, hn_vmem.at[pl.ds(0, n_tok), :], sem.at[0])
        load.start()
        load.wait()

    lo = lo_ref[e * n_blocks + j]
    hi = hi_ref[e * n_blocks + j]
    acc_ref[...] = jnp.zeros(acc_ref.shape, F32)
    slot_id = lax.broadcasted_iota(jnp.int32, (SLOT_BLOCK, TOK_TILE), 0).astype(F32)
    base = (j * SLOT_BLOCK).astype(F32)

    def body(i, carry):
        first = lo + i * GATHER_WINDOW
        pieces = []
        for t in range(GATHER_WINDOW):
            c = first + t
            lp = lp_ref[c, pl.ds(e, 1), :]
            dst = lp + (off_ref[c * N_EXPERTS + e].astype(F32) - base)
            pieces.append(jnp.where((slot_id == dst) & (lp >= 0.0), 1.0, 0.0).astype(BF16))
        onehot = jnp.concatenate(pieces, axis=1)
        start = pl.multiple_of(first * TOK_TILE, TOK_TILE)
        rows = hn_vmem[pl.ds(start, GATHER_WINDOW * TOK_TILE), :]
        acc_ref[...] += jnp.dot(onehot, rows, preferred_element_type=F32)
        return carry

    lax.fori_loop(0, (hi - lo + GATHER_WINDOW) // GATHER_WINDOW, body, 0)
    xs_ref[0] = acc_ref[...].astype(BF16)


def _gather(off, lo, hi, lp_exp, hn, cap):
    n_tok = hn.shape[0]
    n_tiles = lp_exp.shape[0]
    n_blocks = cap // SLOT_BLOCK
    pad = GATHER_WINDOW - 1
    lp_pad = jnp.pad(lp_exp, ((0, pad), (0, 0), (0, 0)), constant_values=-1.0)
    off_pad = jnp.pad(off, ((0, pad), (0, 0))).reshape(-1)
    grid_spec = pltpu.PrefetchScalarGridSpec(
        num_scalar_prefetch=3,
        grid=(N_EXPERTS, n_blocks),
        in_specs=[pl.BlockSpec((n_tiles + pad, N_EXPERTS, TOK_TILE), lambda e, j, *_: (0, 0, 0)),
                  pl.BlockSpec(memory_space=pl.ANY)],
        out_specs=pl.BlockSpec((1, SLOT_BLOCK, D_MODEL), lambda e, j, *_: (e, j, 0)),
        scratch_shapes=[pltpu.VMEM((n_tok + GATHER_PAD_ROWS, D_MODEL), BF16),
                        pltpu.SemaphoreType.DMA((1,)),
                        pltpu.VMEM((SLOT_BLOCK, D_MODEL), F32)],
    )
    return pl.pallas_call(
        functools.partial(_gather_kernel, n_blocks=n_blocks, n_tok=n_tok),
        grid_spec=grid_spec,
        out_shape=jax.ShapeDtypeStruct((N_EXPERTS, cap, D_MODEL), BF16),
        compiler_params=_cparams(("arbitrary", "arbitrary")),
        name="gather",
    )(off_pad, lo, hi, lp_pad, hn)


def _ffn_kernel(xs_ref, wg_ref, wu_ref, wd_ref, eo_ref):
    xs = xs_ref[0]
    g = jnp.dot(xs, wg_ref[0], preferred_element_type=F32)
    u = jnp.dot(xs, wu_ref[0], preferred_element_type=F32)
    hid = (g * jax.nn.sigmoid(g) * u).astype(BF16)
    eo_ref[0] = jnp.dot(hid, wd_ref[0], preferred_element_type=F32).astype(BF16)


def _ffn(xs, w):
    cap = xs.shape[1]
    wspec = lambda shape: pl.BlockSpec(shape, lambda e, j: (e, 0, 0))
    return pl.pallas_call(
        _ffn_kernel,
        grid=(N_EXPERTS, cap // FFN_TM),
        in_specs=[pl.BlockSpec((1, FFN_TM, D_MODEL), lambda e, j: (e, j, 0)),
                  wspec((1, D_MODEL, EXPERT_FF)), wspec((1, D_MODEL, EXPERT_FF)),
                  wspec((1, EXPERT_FF, D_MODEL))],
        out_specs=pl.BlockSpec((1, FFN_TM, D_MODEL), lambda e, j: (e, j, 0)),
        out_shape=jax.ShapeDtypeStruct((N_EXPERTS, cap, D_MODEL), BF16),
        compiler_params=_cparams(("arbitrary", "arbitrary")),
        name="ffn",
    )(xs, w["wg"], w["wu"], w["wd"])


def _combine_kernel(off_ref, x1_ref, aff_ref, lp_ref, eo_hbm, y_ref, buf, sem, *, cap, n_tiles):
    i = pl.program_id(0)

    def window_start(tile, e):
        off = off_ref[tile * N_EXPERTS + e]
        aligned = (off // BF16_SUBLANES) * BF16_SUBLANES
        return pl.multiple_of(jnp.minimum(aligned, cap - READ_WINDOW), BF16_SUBLANES)

    def window_copy(tile, e, slot):
        return pltpu.make_async_copy(
            eo_hbm.at[e, pl.ds(window_start(tile, e), READ_WINDOW), :],
            buf.at[slot, e, pl.ds(0, READ_WINDOW), :], sem.at[slot, e])

    @pl.when(i == 0)
    def _():
        buf[:, :, READ_WINDOW:, :] = jnp.zeros(
            (2, N_EXPERTS, 2 * TOK_TILE - READ_WINDOW, D_MODEL), BF16)
        for e in range(N_EXPERTS):
            window_copy(0, e, 0).start()

    slot = i % 2

    @pl.when(i + 1 < n_tiles)
    def _():
        for e in range(N_EXPERTS):
            window_copy(i + 1, e, 1 - slot).start()

    lp = lp_ref[...]
    aff = aff_ref[...]
    lane = lax.broadcasted_iota(jnp.int32, (TOK_TILE, 2 * TOK_TILE), 1).astype(F32)
    for e in range(N_EXPERTS):
        window_copy(i, e, slot).wait()
    acc = x1_ref[...]
    for e in range(N_EXPERTS):
        shift =(off_ref[i * N_EXPERTS + e] - window_start(i, e)).astype(F32)
        lpe = lp[:, e:e + 1]
        onehot = jnp.where((lane == lpe + shift) & (lpe >= 0.0), 1.0, 0.0).astype(BF16)
        rows = jnp.dot(onehot, buf[slot, e], preferred_element_type=F32)
        acc = acc + aff[:, e:e + 1] * rows
    y_ref[...] = acc


def _combine(off_flat, x1, aff, lp_tok, eo):
    n_tok = x1.shape[0]
    n_tiles = n_tok // TOK_TILE
    cap = eo.shape[1]
    grid_spec = pltpu.PrefetchScalarGridSpec(
        num_scalar_prefetch=1,
        grid=(n_tiles,),
        in_specs=[pl.BlockSpec((TOK_TILE, D_MODEL), lambda i, *_: (i, 0)),
                  pl.BlockSpec((TOK_TILE, LANES), lambda i, *_: (i, 0)),
                  pl.BlockSpec((TOK_TILE, LANES), lambda i, *_: (i, 0)),
                  pl.BlockSpec(memory_space=pl.ANY)],
        out_specs=pl.BlockSpec((TOK_TILE, D_MODEL), lambda i, *_: (i, 0)),
        scratch_shapes=[pltpu.VMEM((2, N_EXPERTS, 2 * TOK_TILE, D_MODEL), BF16),
                        pltpu.SemaphoreType.DMA((2, N_EXPERTS))],
    )
    return pl.pallas_call(
        functools.partial(_combine_kernel, cap=cap, n_tiles=n_tiles),
        grid_spec=grid_spec,
        out_shape=jax.ShapeDtypeStruct((n_tok, D_MODEL), F32),
        compiler_params=_cparams(("arbitrary",)),
        name="combine",
    )(off_flat, x1, aff, lp_tok, eo)


def _rot_cols(w):
    return jnp.concatenate([-w[..., HALF_ROPE:], w[..., :HALF_ROPE]], axis=-1)


def _swap_halves(g):
    return jnp.concatenate([g[HALF_ROPE:], g[:HALF_ROPE]])


def _prep_weights(norm1_g, w_in, v_norm_g, w_spatial, b_spatial, w_a_out, q_norm_g, w_uq,
                  kv_norm_g, w_ukv, q_head_g, k_head_g, w_b_out, w_out, norm2_g,
                  w_router, w_gate, w_up, w_down):
    off_cq = 2 * A_WIDTH
    off_kr = off_cq + Q_LORA_RANK + KV_LORA_RANK
    off_ga = off_kr + QK_ROPE_DIM
    kr = w_in[:, off_kr:off_ga]
    w_in_p = jnp.concatenate([w_in[:, :off_ga], _rot_cols(kr), w_in[:, off_ga:]], axis=1)
    uq = w_uq.reshape(Q_LORA_RANK, N_HEADS, QK_HEAD_DIM)
    uq_rope = uq[..., QK_NOPE_DIM:]
    wuq = jnp.concatenate([uq, _rot_cols(uq_rope)], axis=-1).reshape(Q_LORA_RANK, N_HEADS * QK_PAD)
    ukv = w_ukv.reshape(KV_LORA_RANK, N_HEADS, QK_NOPE_DIM + V_HEAD_DIM)
    wukv = jnp.concatenate([ukv[..., :QK_NOPE_DIM].reshape(KV_LORA_RANK, -1),
                            ukv[..., QK_NOPE_DIM:].reshape(KV_LORA_RANK, -1)], axis=1)
    qg_r = q_head_g[QK_NOPE_DIM:]
    kg_r = k_head_g[QK_NOPE_DIM:]
    return {
        "g1": norm1_g.reshape(1, -1),
        "w_in": w_in_p.astype(BF16),
        "vg": v_norm_g.reshape(1, -1),
        "ws": w_spatial.astype(BF16),
        "bt": b_spatial.T,
        "wa": w_a_out.astype(BF16),
        "qng": q_norm_g.reshape(1, -1),
        "wuq": wuq.astype(BF16),
        "kvg": kv_norm_g.reshape(1, -1),
        "wukv": wukv.astype(BF16),
        "qg": jnp.concatenate([q_head_g, _swap_halves(qg_r)]).reshape(1, -1),
        "kgn": k_head_g[:QK_NOPE_DIM].reshape(1, -1),
        "kg2": jnp.concatenate([kg_r, _swap_halves(kg_r)]).reshape(1, -1),
        "wb": w_b_out.astype(BF16),
        "wo": w_out.astype(BF16),
        "g2": norm2_g.reshape(1, -1),
        "wr": jnp.pad(w_router, ((0, 0), (0, LANES - N_EXPERTS))).astype(BF16),
        "wg": w_gate.astype(BF16),
        "wu": w_up.astype(BF16),
        "wd": w_down.astype(BF16),
    }


def _rope_table(seq):
    inv_freq = ROPE_THETA ** (-jnp.arange(HALF_ROPE, dtype=F32) / HALF_ROPE)
    freqs = jnp.arange(seq, dtype=F32)[:, None] * inv_freq[None, :]
    cos, sin = jnp.cos(freqs), jnp.sin(freqs)
    return jnp.concatenate([cos, cos, sin, sin], axis=1)


def _block_tile_ranges(off, cap):
    n_blocks = cap // SLOT_BLOCK
    off_end = jnp.concatenate([off[1:], jnp.full((1, N_EXPERTS), cap, jnp.int32)], axis=0)
    block_start = jnp.arange(n_blocks, dtype=jnp.int32) * SLOT_BLOCK
    lo = jnp.sum(off_end[:, :, None] <= block_start[None, None, :], axis=0)
    hi = jnp.sum(off[:, :, None] < (block_start + SLOT_BLOCK)[None, None, :], axis=0) - 1
    return lo.astype(jnp.int32).reshape(-1), hi.astype(jnp.int32).reshape(-1)


def _layer_group(x, w):
    batch, seq, _ = x.shape
    n_tok = batch * seq
    cap = max(1, min(n_tok, CAPACITY_FACTOR * n_tok // N_EXPERTS))
    x2d = x.reshape(n_tok, D_MODEL)
    ab, gb, q, k, v = _pre(x2d, _rope_table(seq), w, batch, seq)
    o = _attn(q, k, v, batch, seq)
    x1, hn, aff, aff16 = _post(x2d, o.reshape(n_tok, D_MODEL), ab, gb, w)
    thr, need = _thresh(aff16, cap)
    lp_tok, lp_exp, off = _slots(aff, thr, need)
    off = off[:, 0, :N_EXPERTS].astype(jnp.int32)
    lo, hi = _block_tile_ranges(off, cap)
    off_flat = off.reshape(-1)
    xs = _gather(off, lo, hi, lp_exp, hn, cap)
    eo = _ffn(xs, w)
    y = _combine(off_flat, x1, aff, lp_tok, eo)
    return y.reshape(batch, seq, D_MODEL)


def kernel(x_prompt, x_sample, norm1_g, w_in, v_norm_g, w_spatial, b_spatial, w_a_out, q_norm_g,
           w_uq, kv_norm_g, w_ukv, q_head_g, k_head_g, w_b_out, w_out, norm2_g, w_router,
           w_gate, w_up, w_down):
    params = (norm1_g, w_in, v_norm_g, w_spatial, b_spatial, w_a_out, q_norm_g, w_uq, kv_norm_g,
              w_ukv, q_head_g, k_head_g, w_b_out, w_out, norm2_g, w_router, w_gate, w_up, w_down)
    depth = norm1_g.shape[0]
    y_prompt, y_sample = x_prompt, x_sample
    for l in range(depth):
        w = _prep_weights(*[p[l] for p in params])
        y_prompt = _layer_group(y_prompt, w)
        y_sample = _layer_group(y_sample, w)
    return (y_prompt, y_sample)
```

```python
import functools
import math

import jax
import jax.numpy as jnp
from jax import lax
from jax.experimental import pallas as pl
from jax.experimental.pallas import tpu as pltpu

D_MODEL = 1024
CHUNK = 128
A_WIDTH = 1024
A_GROUPS = 8
A_GROUP_DIM = A_WIDTH // A_GROUPS
N_HEADS = 8
QK_NOPE_DIM = 128
QK_ROPE_DIM = 64
QK_HEAD_DIM = QK_NOPE_DIM + QK_ROPE_DIM
V_HEAD_DIM = 128
Q_LORA_RANK = 256
KV_LORA_RANK = 128
ROPE_THETA = 10000.0
N_EXPERTS = 16
EXPERT_FF = 1024
CAPACITY_FACTOR = 2
NORM_EPS = 1e-6

LANES = 128
BF16_SUBLANES = 16
VMEM_LIMIT_BYTES = 56 * 1024 * 1024

QK_PAD = 2 * LANES
HALF_ROPE = QK_ROPE_DIM // 2

P_UV = 0
P_SMALL = P_UV + 2 * A_WIDTH
P_SMALL_W = Q_LORA_RANK + KV_LORA_RANK + 2 * QK_ROPE_DIM
P_GA = P_SMALL + P_SMALL_W
P_GB = P_GA + D_MODEL
P_COLS = P_GB + D_MODEL

PRE_TM = 256
POST_TM = 512
ATT_TQ = 2048
ATT_TK = 512
ATT_UNROLL = 4
TOK_TILE = 128
SLOTS_TILES = 4
BISECT_STEPS = 160
SLOT_BLOCK = 256
GATHER_WINDOW = 4
GATHER_PAD_ROWS = (GATHER_WINDOW - 1) * TOK_TILE
FFN_TM = 512
READ_WINDOW = TOK_TILE + BF16_SUBLANES

BF16 = jnp.bfloat16
F32 = jnp.float32


def _cparams(sem):
    return pltpu.CompilerParams(dimension_semantics=sem, vmem_limit_bytes=VMEM_LIMIT_BYTES)


def _rms(x, n):
    return lax.rsqrt(jnp.sum(x * x, axis=-1, keepdims=True) * (1.0 / n) + NORM_EPS)


def _const_spec(shape):
    nd = len(shape)
    return pl.BlockSpec(shape, lambda *_: (0,) * nd)


def _pre_kernel(x_ref, cs_ref, g1_ref, win_ref, vg_ref, ws_ref, bt_ref, wa_ref,
                qng_ref, wuq_ref, kvg_ref, wukv_ref, qg_ref, kgn_ref, kg2_ref,
                ab_ref, gb_ref, q_ref, k_ref, v_ref, sv_ref):
    x = x_ref[...]
    h = (x * _rms(x, D_MODEL) * g1_ref[...]).astype(BF16)

    uv = jnp.dot(h, win_ref[:, P_UV:P_SMALL], preferred_element_type=F32)
    uv = 0.5 * uv * (1.0 + lax.erf(uv * (1.0 / math.sqrt(2.0))))
    u = uv[:, :A_WIDTH]
    v = uv[:, A_WIDTH:]
    vn = (v * _rms(v, A_WIDTH) * vg_ref[...]).astype(BF16)
    for n in range(PRE_TM // CHUNK):
        for g in range(A_GROUPS):
            blk = vn[n * CHUNK:(n + 1) * CHUNK, g * A_GROUP_DIM:(g + 1) * A_GROUP_DIM]
            sv = jnp.dot(ws_ref[g], blk, preferred_element_type=F32) + bt_ref[:, g:g + 1]
            sv_ref[n * CHUNK:(n + 1) * CHUNK, g * A_GROUP_DIM:(g + 1) * A_GROUP_DIM] = sv
    a = jnp.dot((u * sv_ref[...]).astype(BF16), wa_ref[...], preferred_element_type=F32)
    ga = jax.nn.sigmoid(jnp.dot(h, win_ref[:, P_GA:P_GB], preferred_element_type=F32))
    ab_ref[...] = ga * a
    gb_ref[...] = jax.nn.sigmoid(jnp.dot(h, win_ref[:, P_GB:P_COLS], preferred_element_type=F32))

    small = jnp.dot(h, win_ref[:, P_SMALL:P_GA], preferred_element_type=F32)
    cq = small[:, :Q_LORA_RANK]
    ckv = small[:, Q_LORA_RANK:Q_LORA_RANK + KV_LORA_RANK]
    kr = small[:, Q_LORA_RANK + KV_LORA_RANK:]
    cs = cs_ref[...]
    lane = lax.broadcasted_iota(jnp.int32, (PRE_TM, LANES), 1)
    rope_lanes = lane < QK_ROPE_DIM

    cqn = (cq * _rms(cq, Q_LORA_RANK) * qng_ref[...]).astype(BF16)
    qf = jnp.dot(cqn, wuq_ref[...], preferred_element_type=F32)
    ckvn = (ckv * _rms(ckv, KV_LORA_RANK) * kvg_ref[...]).astype(BF16)
    kvf = jnp.dot(ckvn, wukv_ref[...], preferred_element_type=F32)

    kr_ss = jnp.sum(jnp.where(rope_lanes, kr * kr, 0.0), axis=-1, keepdims=True)
    t = kr * kg2_ref[...] * cs
    kro = jnp.where(rope_lanes, t + pltpu.roll(t, QK_ROPE_DIM, 1), 0.0)

    qg = qg_ref[...]
    scale = QK_HEAD_DIM ** -0.5 * math.log2(math.e)
    for hd in range(N_HEADS):
        qn = qf[:, hd * QK_PAD:hd * QK_PAD + LANES]
        q2 = qf[:, hd * QK_PAD + LANES:(hd + 1) * QK_PAD]
        q_ss = (jnp.sum(qn * qn, axis=-1, keepdims=True)
                + jnp.sum(jnp.where(rope_lanes, q2 * q2, 0.0), axis=-1, keepdims=True))
        q_rs = lax.rsqrt(q_ss * (1.0 / QK_HEAD_DIM) + NORM_EPS) * scale
        t = q2 * qg[:, LANES:] * cs
        q_ref[0, hd, :, :LANES] = (qn * qg[:, :LANES] * q_rs).astype(BF16)
        q_ref[0, hd, :, LANES:] = ((t + pltpu.roll(t, QK_ROPE_DIM, 1)) * q_rs).astype(BF16)

        kn = kvf[:, hd * QK_NOPE_DIM:(hd + 1) * QK_NOPE_DIM]
        k_ss = jnp.sum(kn * kn, axis=-1, keepdims=True) + kr_ss
        k_rs = lax.rsqrt(k_ss * (1.0 / QK_HEAD_DIM) + NORM_EPS)
        k_ref[0, hd, :, :LANES] = (kn * kgn_ref[...] * k_rs).astype(BF16)
        k_ref[0, hd, :, LANES:] = (kro * k_rs).astype(BF16)
        v_ref[0, hd] = kvf[:, N_HEADS * QK_NOPE_DIM + hd * V_HEAD_DIM:
                           N_HEADS * QK_NOPE_DIM + (hd + 1) * V_HEAD_DIM].astype(BF16)


def _pre(x2d, cs, w, batch, seq):
    n_tok = batch * seq
    tiles_per_seq = seq // PRE_TM
    tok = lambda i: (i, 0)
    head = lambda i: (i // tiles_per_seq, 0, i % tiles_per_seq, 0)
    in_specs = [
        pl.BlockSpec((PRE_TM, D_MODEL), tok),
        pl.BlockSpec((PRE_TM, LANES), lambda i: (i % tiles_per_seq, 0)),
        _const_spec((1, D_MODEL)),
        _const_spec((D_MODEL, P_COLS)),
        _const_spec((1, A_WIDTH)),
        _const_spec((A_GROUPS, CHUNK, CHUNK)),
        _const_spec((CHUNK, A_GROUPS)),
        _const_spec((A_WIDTH, D_MODEL)),
        _const_spec((1, Q_LORA_RANK)),
        _const_spec((Q_LORA_RANK, N_HEADS * QK_PAD)),
        _const_spec((1, KV_LORA_RANK)),
        _const_spec((KV_LORA_RANK, N_HEADS * (QK_NOPE_DIM + V_HEAD_DIM))),
        _const_spec((1, QK_PAD)),
        _const_spec((1, LANES)),
        _const_spec((1, LANES)),
    ]
    out_specs = [
        pl.BlockSpec((PRE_TM, D_MODEL), tok),
        pl.BlockSpec((PRE_TM, D_MODEL), tok),
        pl.BlockSpec((1, N_HEADS, PRE_TM, QK_PAD), head),
        pl.BlockSpec((1, N_HEADS, PRE_TM, QK_PAD), head),
        pl.BlockSpec((1, N_HEADS, PRE_TM, V_HEAD_DIM), head),
    ]
    out_shape = [
        jax.ShapeDtypeStruct((n_tok, D_MODEL), F32),
        jax.ShapeDtypeStruct((n_tok, D_MODEL), F32),
        jax.ShapeDtypeStruct((batch, N_HEADS, seq, QK_PAD), BF16),
        jax.ShapeDtypeStruct((batch, N_HEADS, seq, QK_PAD), BF16),
        jax.ShapeDtypeStruct((batch, N_HEADS, seq, V_HEAD_DIM), BF16),
    ]
    return pl.pallas_call(
        _pre_kernel,
        grid=(n_tok // PRE_TM,),
        in_specs=in_specs,
        out_specs=out_specs,
        out_shape=out_shape,
        scratch_shapes=[pltpu.VMEM((PRE_TM, A_WIDTH), F32)],
        compiler_params=_cparams(("arbitrary",)),
        name="pre",
    )(x2d, cs, w["g1"], w["w_in"], w["vg"], w["ws"], w["bt"], w["wa"], w["qng"], w["wuq"],
      w["kvg"], w["wukv"], w["qg"], w["kgn"], w["kg2"])


def _attn_kernel(q_ref, k_ref, v_ref, o_ref, m_ref, l_ref, acc_ref, *, seq):
    q = q_ref[0, 0]
    m_ref[...] = jnp.full(m_ref.shape, -jnp.inf, F32)
    l_ref[...] = jnp.zeros(l_ref.shape, F32)
    acc_ref[...] = jnp.zeros(acc_ref.shape, F32)
    n_chunks = ATT_TK // LANES

    def body(j, carry):
        start = pl.multiple_of(j * ATT_TK, ATT_TK)
        kt = k_ref[0, 0, pl.ds(start, ATT_TK), :]
        vt = v_ref[0, 0, pl.ds(start, ATT_TK), :]
        s = lax.dot_general(q, kt, (((1,), (1,)), ((), ())), preferred_element_type=F32)
        chunks = [s[:, c * LANES:(c + 1) * LANES] for c in range(n_chunks)]
        m_old = m_ref[...]
        row_max = jnp.max(functools.reduce(jnp.maximum, chunks), axis=-1, keepdims=True)
        m_new = jnp.maximum(m_old, row_max)
        alpha = jnp.exp2(m_old - m_new)
        ps = [jnp.exp2(c - m_new) for c in chunks]
        l_ref[...] = alpha * l_ref[...] + functools.reduce(jnp.add, ps)
        p = jnp.concatenate([x.astype(BF16) for x in ps], axis=1)
        acc_ref[...] = alpha * acc_ref[...] + jnp.dot(p, vt, preferred_element_type=F32)
        m_ref[...] = m_new
        return carry

    lax.fori_loop(0, seq // ATT_TK, body, 0, unroll=ATT_UNROLL)
    l = jnp.sum(l_ref[...], axis=-1, keepdims=True)
    o_ref[0] = (acc_ref[...] / l).astype(BF16)


def _attn(q, k, v, batch, seq):
    return pl.pallas_call(
        functools.partial(_attn_kernel, seq=seq),
        grid=(batch, N_HEADS, seq // ATT_TQ),
        in_specs=[
            pl.BlockSpec((1, 1, ATT_TQ, QK_PAD), lambda b, h, i: (b, h, i, 0)),
            pl.BlockSpec((1, 1, seq, QK_PAD), lambda b, h, i: (b, h, 0, 0)),
            pl.BlockSpec((1, 1, seq, V_HEAD_DIM), lambda b, h, i: (b, h, 0, 0)),
        ],
        out_specs=pl.BlockSpec((1, ATT_TQ, V_HEAD_DIM), lambda b, h, i: (b, i, h)),
        out_shape=jax.ShapeDtypeStruct((batch, seq, N_HEADS * V_HEAD_DIM), BF16),
        scratch_shapes=[pltpu.VMEM((ATT_TQ, LANES), F32), pltpu.VMEM((ATT_TQ, LANES), F32),
                        pltpu.VMEM((ATT_TQ, V_HEAD_DIM), F32)],
        compiler_params=_cparams(("arbitrary", "arbitrary", "arbitrary")),
        name="attn",
    )(q, k, v)


def _post_kernel(x_ref, o_ref, ab_ref, gb_ref, wb_ref, wo_ref, g2_ref, wr_ref,
                 x1_ref, hn_ref, aff_ref, aff16_ref):
    b = jnp.dot(o_ref[...], wb_ref[...], preferred_element_type=F32)
    mix = ab_ref[...] + gb_ref[...] * b
    x1 = x_ref[...] + jnp.dot(mix.astype(BF16), wo_ref[...], preferred_element_type=F32)
    x1_ref[...] = x1
    hn = (x1 * _rms(x1, D_MODEL) * g2_ref[...]).astype(BF16)
    hn_ref[...] = hn
    logits = jnp.dot(hn, wr_ref[...], preferred_element_type=F32)
    lane = lax.broadcasted_iota(jnp.int32, logits.shape, 1)
    logits = jnp.where(lane < N_EXPERTS, logits, -jnp.inf)
    e = jnp.exp(logits - jnp.max(logits, axis=-1, keepdims=True))
    aff = e / jnp.sum(e, axis=-1, keepdims=True)
    aff_ref[...] = aff
    aff16_ref[...] = aff[:, :N_EXPERTS]


def _post(x2d, o2d, ab, gb, w):
    n_tok = x2d.shape[0]
    tok = lambda i: (i, 0)
    return pl.pallas_call(
        _post_kernel,
        grid=(n_tok // POST_TM,),
        in_specs=[
            pl.BlockSpec((POST_TM, D_MODEL), tok),
            pl.BlockSpec((POST_TM, D_MODEL), tok),
            pl.BlockSpec((POST_TM, D_MODEL), tok),
            pl.BlockSpec((POST_TM, D_MODEL), tok),
            _const_spec((N_HEADS * V_HEAD_DIM, D_MODEL)),
            _const_spec((D_MODEL, D_MODEL)),
            _const_spec((1, D_MODEL)),
            _const_spec((D_MODEL, LANES)),
        ],
        out_specs=[
            pl.BlockSpec((POST_TM, D_MODEL), tok),
            pl.BlockSpec((POST_TM, D_MODEL), tok),
            pl.BlockSpec((POST_TM, LANES), tok),
            pl.BlockSpec((POST_TM, N_EXPERTS), tok),
        ],
        out_shape=[
            jax.ShapeDtypeStruct((n_tok, D_MODEL), F32),
            jax.ShapeDtypeStruct((n_tok, D_MODEL), BF16),
            jax.ShapeDtypeStruct((n_tok, LANES), F32),
            jax.ShapeDtypeStruct((n_tok, N_EXPERTS), F32),
        ],
        compiler_params=_cparams(("arbitrary",)),
        name="post",
    )(x2d, o2d, ab, gb, w["wb"], w["wo"], w["g2"], w["wr"])


def _thresh_kernel(aff_ref, lo_ref, hi_ref, need_ref, *, cap):
    aff = aff_ref[...]

    def per_expert(c):
        for shift in (64, 32, 16):
            c = c + pltpu.roll(c, shift, 1)
        return c

    def count_ge(x):
        return per_expert(jnp.sum(jnp.where(aff >= x, 1.0, 0.0), axis=0, keepdims=True))

    def body(i, carry):
        lo, hi = carry
        mid = 0.5 * (lo + hi)
        take = count_ge(mid) >= cap
        return jnp.where(take, mid, lo), jnp.where(take, hi, mid)

    lo, hi = lax.fori_loop(0, BISECT_STEPS, body,
                           (jnp.zeros((1, LANES), F32), jnp.full((1, LANES), 2.0, F32)))
    lo_ref[...] = lo
    hi_ref[...] = hi
    need_ref[...] = cap - count_ge(hi)


def _thresh(aff16, cap):
    n_tok = aff16.shape[0]
    packed = aff16.reshape(n_tok * N_EXPERTS // LANES, LANES)
    row = jax.ShapeDtypeStruct((1, LANES), F32)
    lo, hi, need = pl.pallas_call(
        functools.partial(_thresh_kernel, cap=float(cap)),
        out_shape=[row, row, row],
        compiler_params=pltpu.CompilerParams(vmem_limit_bytes=VMEM_LIMIT_BYTES),
        name="thresh",
    )(packed)
    lane = jnp.arange(LANES)
    lo = jnp.where(lane < N_EXPERTS, lo, jnp.inf)
    hi = jnp.where(lane < N_EXPERTS, hi, jnp.inf)
    need = jnp.where(lane < N_EXPERTS, need, 0.0)
    return lo, hi, need


def _slots_kernel(aff_ref, lo_ref, hi_ref, need_ref, lp_tok_ref, lp_exp_ref, off_ref, run_sel, run_eq):
    @pl.when(pl.program_id(0) == 0)
    def _():
        run_sel[...] = jnp.zeros(run_sel.shape, F32)
        run_eq[...] = jnp.zeros(run_eq.shape, F32)

    lo = lo_ref[...]
    hi = hi_ref[...]
    row = lax.broadcasted_iota(jnp.int32, (TOK_TILE, TOK_TILE), 0)
    col = lax.broadcasted_iota(jnp.int32, (TOK_TILE, TOK_TILE), 1)
    earlier = jnp.where(col < row, 1.0, 0.0).astype(BF16)
    for t in range(SLOTS_TILES):
        aff = aff_ref[t * TOK_TILE:(t + 1) * TOK_TILE, :]
        above = aff >= hi
        eq = (aff >= lo) & jnp.logical_not(above)
        eqf = jnp.where(eq, 1.0, 0.0)
        eq_rank = jnp.dot(earlier, eqf.astype(BF16), preferred_element_type=F32) + run_eq[...]
        sel = above | (eq & (eq_rank < need_ref[...]))
        self_ = jnp.where(sel, 1.0, 0.0)
        lpos = jnp.dot(earlier, self_.astype(BF16), preferred_element_type=F32)
        lp = jnp.where(sel, lpos, -1.0)
        lp_tok_ref[t * TOK_TILE:(t + 1) * TOK_TILE, :] = lp
        lp_exp_ref[t] = lp.T[:N_EXPERTS, :]
        off_ref[t] = run_sel[...]
        run_sel[...] = run_sel[...] + jnp.sum(self_, axis=0, keepdims=True)
        run_eq[...] = run_eq[...] + jnp.sum(eqf, axis=0, keepdims=True)


def _slots(aff, lo, hi, need):
    n_tok = aff.shape[0]
    n_tiles = n_tok // TOK_TILE
    return pl.pallas_call(
        _slots_kernel,
        grid=(n_tiles // SLOTS_TILES,),
        in_specs=[pl.BlockSpec((SLOTS_TILES * TOK_TILE, LANES), lambda i: (i, 0)),
                  _const_spec((1, LANES)), _const_spec((1, LANES)), _const_spec((1, LANES))],
        out_specs=[pl.BlockSpec((SLOTS_TILES * TOK_TILE, LANES), lambda i: (i, 0)),
                   pl.BlockSpec((SLOTS_TILES, N_EXPERTS, TOK_TILE), lambda i: (i, 0, 0)),
                   pl.BlockSpec((SLOTS_TILES, 1, LANES), lambda i: (i, 0, 0))],
        out_shape=[jax.ShapeDtypeStruct((n_tok, LANES), F32),
                   jax.ShapeDtypeStruct((n_tiles, N_EXPERTS, TOK_TILE), F32),
                   jax.ShapeDtypeStruct((n_tiles, 1, LANES), F32)],
        scratch_shapes=[pltpu.VMEM((1, LANES), F32), pltpu.VMEM((1, LANES), F32)],
        compiler_params=_cparams(("arbitrary",)),
        name="slots",
    )(aff, lo, hi, need)


def _gather_kernel(off_ref, lo_ref, hi_ref, lp_ref, hn_hbm, xs_ref, hn_vmem, sem, acc_ref, *,
                   n_blocks, n_tok):
    e = pl.program_id(0)
    j = pl.program_id(1)

    @pl.when((e == 0) & (j == 0))
    def _():
        hn_vmem[pl.ds(n_tok, GATHER_PAD_ROWS), :] = jnp.zeros((GATHER_PAD_ROWS, D_MODEL), BF16)
        load = pltpu.make_async_copy(hn_hbm, hn_vmem.at[pl.ds(0, n_tok), :], sem.at[0])
        load.start()
        load.wait()

    lo = lo_ref[e * n_blocks + j]
    hi = hi_ref[e * n_blocks + j]
    acc_ref[...] = jnp.zeros(acc_ref.shape, F32)
    slot_id = lax.broadcasted_iota(jnp.int32, (SLOT_BLOCK, TOK_TILE), 0).astype(F32)
    base = (j * SLOT_BLOCK).astype(F32)

    def body(i, carry):
        first = lo + i * GATHER_WINDOW
        pieces = []
        for t in range(GATHER_WINDOW):
            c = first + t
            lp = lp_ref[c, pl.ds(e, 1), :]
            dst = lp + (off_ref[c * N_EXPERTS + e].astype(F32) - base)
            pieces.append(jnp.where((slot_id == dst) & (lp >= 0.0), 1.0, 0.0).astype(BF16))
        onehot = jnp.concatenate(pieces, axis=1)
        start = pl.multiple_of(first * TOK_TILE, TOK_TILE)
        rows = hn_vmem[pl.ds(start, GATHER_WINDOW * TOK_TILE), :]
        acc_ref[...] += jnp.dot(onehot, rows, preferred_element_type=F32)
        return carry

    lax.fori_loop(0, (hi - lo + GATHER_WINDOW) // GATHER_WINDOW, body, 0)
    xs_ref[0] = acc_ref[...].astype(BF16)


def _gather(off, lo, hi, lp_exp, hn, cap):
    n_tok = hn.shape[0]
    n_tiles = lp_exp.shape[0]
    n_blocks = cap // SLOT_BLOCK
    pad = GATHER_WINDOW - 1
    lp_pad = jnp.pad(lp_exp, ((0, pad), (0, 0), (0, 0)), constant_values=-1.0)
    off_pad = jnp.pad(off, ((0, pad), (0, 0))).reshape(-1)
    grid_spec = pltpu.PrefetchScalarGridSpec(
        num_scalar_prefetch=3,
        grid=(N_EXPERTS, n_blocks),
        in_specs=[pl.BlockSpec((n_tiles + pad, N_EXPERTS, TOK_TILE), lambda e, j, *_: (0, 0, 0)),
                  pl.BlockSpec(memory_space=pl.ANY)],
        out_specs=pl.BlockSpec((1, SLOT_BLOCK, D_MODEL), lambda e, j, *_: (e, j, 0)),
        scratch_shapes=[pltpu.VMEM((n_tok + GATHER_PAD_ROWS, D_MODEL), BF16),
                        pltpu.SemaphoreType.DMA((1,)),
                        pltpu.VMEM((SLOT_BLOCK, D_MODEL), F32)],
    )
    return pl.pallas_call(
        functools.partial(_gather_kernel, n_blocks=n_blocks, n_tok=n_tok),
        grid_spec=grid_spec,
        out_shape=jax.ShapeDtypeStruct((N_EXPERTS, cap, D_MODEL), BF16),
        compiler_params=_cparams(("arbitrary", "arbitrary")),
        name="gather",
    )(off_pad, lo, hi, lp_pad, hn)


def _ffn_kernel(xs_ref, wg_ref, wu_ref, wd_ref, eo_ref):
    xs = xs_ref[0]
    g = jnp.dot(xs, wg_ref[0], preferred_element_type=F32)
    u = jnp.dot(xs, wu_ref[0], preferred_element_type=F32)
    hid = (g * jax.nn.sigmoid(g) * u).astype(BF16)
    eo_ref[0] = jnp.dot(hid, wd_ref[0], preferred_element_type=F32).astype(BF16)


def _ffn(xs, w):
    cap = xs.shape[1]
    wspec = lambda shape: pl.BlockSpec(shape, lambda e, j: (e, 0, 0))
    return pl.pallas_call(
        _ffn_kernel,
        grid=(N_EXPERTS, cap // FFN_TM),
        in_specs=[pl.BlockSpec((1, FFN_TM, D_MODEL), lambda e, j: (e, j, 0)),
                  wspec((1, D_MODEL, EXPERT_FF)), wspec((1, D_MODEL, EXPERT_FF)),
                  wspec((1, EXPERT_FF, D_MODEL))],
        out_specs=pl.BlockSpec((1, FFN_TM, D_MODEL), lambda e, j: (e, j, 0)),
        out_shape=jax.ShapeDtypeStruct((N_EXPERTS, cap, D_MODEL), BF16),
        compiler_params=_cparams(("arbitrary", "arbitrary")),
        name="ffn",
    )(xs, w["wg"], w["wu"], w["wd"])


def _combine_kernel(off_ref, x1_ref, aff_ref, lp_ref, eo_hbm, y_ref, buf, sem, *, cap, n_tiles):
    i = pl.program_id(0)

    def window_start(tile, e):
        off = off_ref[tile * N_EXPERTS + e]
        aligned = (off // BF16_SUBLANES) * BF16_SUBLANES
        return pl.multiple_of(jnp.minimum(aligned, cap - READ_WINDOW), BF16_SUBLANES)

    def window_copy(tile, e, slot):
        return pltpu.make_async_copy(
            eo_hbm.at[e, pl.ds(window_start(tile, e), READ_WINDOW), :],
            buf.at[slot, e, pl.ds(0, READ_WINDOW), :], sem.at[slot, e])

    @pl.when(i == 0)
    def _():
        buf[:, :, READ_WINDOW:, :] = jnp.zeros(
            (2, N_EXPERTS, 2 * TOK_TILE - READ_WINDOW, D_MODEL), BF16)
        for e in range(N_EXPERTS):
            window_copy(0, e, 0).start()

    slot = i % 2

    @pl.when(i + 1 < n_tiles)
    def _():
        for e in range(N_EXPERTS):
            window_copy(i + 1, e, 1 - slot).start()

    lp = lp_ref[...]
    aff = aff_ref[...]
    lane = lax.broadcasted_iota(jnp.int32, (TOK_TILE, 2 * TOK_TILE), 1).astype(F32)
    for e in range(N_EXPERTS):
        window_copy(i, e, slot).wait()
    acc = x1_ref[...]
    for e in range(N_EXPERTS):
        shift =(off_ref[i * N_EXPERTS + e] - window_start(i, e)).astype(F32)
        lpe = lp[:, e:e + 1]
        onehot = jnp.where((lane == lpe + shift) & (lpe >= 0.0), 1.0, 0.0).astype(BF16)
        rows = jnp.dot(onehot, buf[slot, e], preferred_element_type=F32)
        acc = acc + aff[:, e:e + 1] * rows
    y_ref[...] = acc


def _combine(off_flat, x1, aff, lp_tok, eo):
    n_tok = x1.shape[0]
    n_tiles = n_tok // TOK_TILE
    cap = eo.shape[1]
    grid_spec = pltpu.PrefetchScalarGridSpec(
        num_scalar_prefetch=1,
        grid=(n_tiles,),
        in_specs=[pl.BlockSpec((TOK_TILE, D_MODEL), lambda i, *_: (i, 0)),
                  pl.BlockSpec((TOK_TILE, LANES), lambda i, *_: (i, 0)),
                  pl.BlockSpec((TOK_TILE, LANES), lambda i, *_: (i, 0)),
                  pl.BlockSpec(memory_space=pl.ANY)],
        out_specs=pl.BlockSpec((TOK_TILE, D_MODEL), lambda i, *_: (i, 0)),
        scratch_shapes=[pltpu.VMEM((2, N_EXPERTS, 2 * TOK_TILE, D_MODEL), BF16),
                        pltpu.SemaphoreType.DMA((2, N_EXPERTS))],
    )
    return pl.pallas_call(
        functools.partial(_combine_kernel, cap=cap, n_tiles=n_tiles),
        grid_spec=grid_spec,
        out_shape=jax.ShapeDtypeStruct((n_tok, D_MODEL), F32),
        compiler_params=_cparams(("arbitrary",)),
        name="combine",
    )(off_flat, x1, aff, lp_tok, eo)


def _rot_cols(w):
    return jnp.concatenate([-w[..., HALF_ROPE:], w[..., :HALF_ROPE]], axis=-1)


def _swap_halves(g):
    return jnp.concatenate([g[HALF_ROPE:], g[:HALF_ROPE]])


def _prep_weights(norm1_g, w_in, v_norm_g, w_spatial, b_spatial, w_a_out, q_norm_g, w_uq,
                  kv_norm_g, w_ukv, q_head_g, k_head_g, w_b_out, w_out, norm2_g,
                  w_router, w_gate, w_up, w_down):
    off_cq = 2 * A_WIDTH
    off_kr = off_cq + Q_LORA_RANK + KV_LORA_RANK
    off_ga = off_kr + QK_ROPE_DIM
    kr = w_in[:, off_kr:off_ga]
    w_in_p = jnp.concatenate([w_in[:, :off_ga], _rot_cols(kr), w_in[:, off_ga:]], axis=1)
    uq = w_uq.reshape(Q_LORA_RANK, N_HEADS, QK_HEAD_DIM)
    uq_rope = uq[..., QK_NOPE_DIM:]
    wuq = jnp.concatenate([uq, _rot_cols(uq_rope)], axis=-1).reshape(Q_LORA_RANK, N_HEADS * QK_PAD)
    ukv = w_ukv.reshape(KV_LORA_RANK, N_HEADS, QK_NOPE_DIM + V_HEAD_DIM)
    wukv = jnp.concatenate([ukv[..., :QK_NOPE_DIM].reshape(KV_LORA_RANK, -1),
                            ukv[..., QK_NOPE_DIM:].reshape(KV_LORA_RANK, -1)], axis=1)
    qg_r = q_head_g[QK_NOPE_DIM:]
    kg_r = k_head_g[QK_NOPE_DIM:]
    return {
        "g1": norm1_g.reshape(1, -1),
        "w_in": w_in_p.astype(BF16),
        "vg": v_norm_g.reshape(1, -1),
        "ws": w_spatial.astype(BF16),
        "bt": b_spatial.T,
        "wa": w_a_out.astype(BF16),
        "qng": q_norm_g.reshape(1, -1),
        "wuq": wuq.astype(BF16),
        "kvg": kv_norm_g.reshape(1, -1),
        "wukv": wukv.astype(BF16),
        "qg": jnp.concatenate([q_head_g, _swap_halves(qg_r)]).reshape(1, -1),
        "kgn": k_head_g[:QK_NOPE_DIM].reshape(1, -1),
        "kg2": jnp.concatenate([kg_r, _swap_halves(kg_r)]).reshape(1, -1),
        "wb": w_b_out.astype(BF16),
        "wo": w_out.astype(BF16),
        "g2": norm2_g.reshape(1, -1),
        "wr": jnp.pad(w_router, ((0, 0), (0, LANES - N_EXPERTS))).astype(BF16),
        "wg": w_gate.astype(BF16),
        "wu": w_up.astype(BF16),
        "wd": w_down.astype(BF16),
    }


def _rope_table(seq):
    inv_freq = ROPE_THETA ** (-jnp.arange(HALF_ROPE, dtype=F32) / HALF_ROPE)
    freqs = jnp.arange(seq, dtype=F32)[:, None] * inv_freq[None, :]
    cos, sin = jnp.cos(freqs), jnp.sin(freqs)
    return jnp.concatenate([cos, cos, sin, sin], axis=1)


def _block_tile_ranges(off, cap):
    n_blocks = cap // SLOT_BLOCK
    off_end = jnp.concatenate([off[1:], jnp.full((1, N_EXPERTS), cap, jnp.int32)], axis=0)
    block_start = jnp.arange(n_blocks, dtype=jnp.int32) * SLOT_BLOCK
    lo = jnp.sum(off_end[:, :, None] <= block_start[None, None, :], axis=0)
    hi = jnp.sum(off[:, :, None] < (block_start + SLOT_BLOCK)[None, None, :], axis=0) - 1
    return lo.astype(jnp.int32).reshape(-1), hi.astype(jnp.int32).reshape(-1)


def _layer_group(x, w):
    batch, seq, _ = x.shape
    n_tok = batch * seq
    cap = max(1, min(n_tok, CAPACITY_FACTOR * n_tok // N_EXPERTS))
    x2d = x.reshape(n_tok, D_MODEL)
    ab, gb, q, k, v = _pre(x2d, _rope_table(seq), w, batch, seq)
    o = _attn(q, k, v, batch, seq)
    x1, hn, aff, aff16 = _post(x2d, o.reshape(n_tok, D_MODEL), ab, gb, w)
    lo_thr, hi_thr, need = _thresh(aff16, cap)
    lp_tok, lp_exp, off = _slots(aff, lo_thr, hi_thr, need)
    off = off[:, 0, :N_EXPERTS].astype(jnp.int32)
    lo, hi = _block_tile_ranges(off, cap)
    off_flat = off.reshape(-1)
    xs = _gather(off, lo, hi, lp_exp, hn, cap)
    eo = _ffn(xs, w)
    y = _combine(off_flat, x1, aff, lp_tok, eo)
    return y.reshape(batch, seq, D_MODEL)


def kernel(x_prompt, x_sample, norm1_g, w_in, v_norm_g, w_spatial, b_spatial, w_a_out, q_norm_g,
           w_uq, kv_norm_g, w_ukv, q_head_g, k_head_g, w_b_out, w_out, norm2_g, w_router,
           w_gate, w_up, w_down):
    params = (norm1_g, w_in, v_norm_g, w_spatial, b_spatial, w_a_out, q_norm_g, w_uq, kv_norm_g,
              w_ukv, q_head_g, k_head_g, w_b_out, w_out, norm2_g, w_router, w_gate, w_up, w_down)
    depth = norm1_g.shape[0]
    y_prompt, y_sample = x_prompt, x_sample
    for l in range(depth):
        w = _prep_weights(*[p[l] for p in params])
        y_prompt = _layer_group(y_prompt, w)
        y_sample = _layer_group(y_sample, w)
    return (y_prompt, y_sample)
```

```python
import functools
import math

import jax
import jax.numpy as jnp
from jax import lax
from jax.experimental import pallas as pl
from jax.experimental.pallas import tpu as pltpu

D_MODEL = 1024
CHUNK = 128
A_WIDTH = 1024
A_GROUPS = 8
A_GROUP_DIM = A_WIDTH // A_GROUPS
N_HEADS = 8
QK_NOPE_DIM = 128
QK_ROPE_DIM = 64
QK_HEAD_DIM = QK_NOPE_DIM + QK_ROPE_DIM
V_HEAD_DIM = 128
Q_LORA_RANK = 256
KV_LORA_RANK = 128
ROPE_THETA = 10000.0
N_EXPERTS = 16
EXPERT_FF = 1024
CAPACITY_FACTOR = 2
NORM_EPS = 1e-6

LANES = 128
BF16_SUBLANES = 16
VMEM_LIMIT_BYTES = 56 * 1024 * 1024

QK_PAD = 2 * LANES
HALF_ROPE = QK_ROPE_DIM // 2

P_UV = 0
P_SMALL = P_UV + 2 * A_WIDTH
P_SMALL_W = Q_LORA_RANK + KV_LORA_RANK + 2 * QK_ROPE_DIM
P_GA = P_SMALL + P_SMALL_W
P_GB = P_GA + D_MODEL
P_COLS = P_GB + D_MODEL

PRE_TM = 256
POST_TM = 512
ATT_TQ = 2048
ATT_TK = 512
ATT_UNROLL = 4
TOK_TILE = 128
SLOTS_TILES = 4
BISECT_STEPS = 84
SLOT_BLOCK = 256
GATHER_WINDOW = 4
GATHER_PAD_ROWS = (GATHER_WINDOW - 1) * TOK_TILE
FFN_TM = 512
READ_WINDOW = TOK_TILE + BF16_SUBLANES

BF16 = jnp.bfloat16
F32 = jnp.float32


def _cparams(sem):
    return pltpu.CompilerParams(dimension_semantics=sem, vmem_limit_bytes=VMEM_LIMIT_BYTES)


def _rms(x, n):
    return lax.rsqrt(jnp.sum(x * x, axis=-1, keepdims=True) * (1.0 / n) + NORM_EPS)


def _const_spec(shape):
    nd = len(shape)
    return pl.BlockSpec(shape, lambda *_: (0,) * nd)


def _pre_kernel(x_ref, cs_ref, g1_ref, win_ref, vg_ref, ws_ref, bt_ref, wa_ref,
                qng_ref, wuq_ref, kvg_ref, wukv_ref, qg_ref, kgn_ref, kg2_ref,
                ab_ref, gb_ref, q_ref, k_ref, v_ref, sv_ref):
    x = x_ref[...]
    h = (x * _rms(x, D_MODEL) * g1_ref[...]).astype(BF16)

    uv = jnp.dot(h, win_ref[:, P_UV:P_SMALL], preferred_element_type=F32)
    uv = 0.5 * uv * (1.0 + lax.erf(uv * (1.0 / math.sqrt(2.0))))
    u = uv[:, :A_WIDTH]
    v = uv[:, A_WIDTH:]
    vn = (v * _rms(v, A_WIDTH) * vg_ref[...]).astype(BF16)
    for n in range(PRE_TM // CHUNK):
        for g in range(A_GROUPS):
            blk = vn[n * CHUNK:(n + 1) * CHUNK, g * A_GROUP_DIM:(g + 1) * A_GROUP_DIM]
            sv = jnp.dot(ws_ref[g], blk, preferred_element_type=F32) + bt_ref[:, g:g + 1]
            sv_ref[n * CHUNK:(n + 1) * CHUNK, g * A_GROUP_DIM:(g + 1) * A_GROUP_DIM] = sv
    a = jnp.dot((u * sv_ref[...]).astype(BF16), wa_ref[...], preferred_element_type=F32)
    ga = jax.nn.sigmoid(jnp.dot(h, win_ref[:, P_GA:P_GB], preferred_element_type=F32))
    ab_ref[...] = ga * a
    gb_ref[...] = jax.nn.sigmoid(jnp.dot(h, win_ref[:, P_GB:P_COLS], preferred_element_type=F32))

    small = jnp.dot(h, win_ref[:, P_SMALL:P_GA], preferred_element_type=F32)
    cq = small[:, :Q_LORA_RANK]
    ckv = small[:, Q_LORA_RANK:Q_LORA_RANK + KV_LORA_RANK]
    kr = small[:, Q_LORA_RANK + KV_LORA_RANK:]
    cs = cs_ref[...]
    lane = lax.broadcasted_iota(jnp.int32, (PRE_TM, LANES), 1)
    rope_lanes = lane < QK_ROPE_DIM

    cqn = (cq * _rms(cq, Q_LORA_RANK) * qng_ref[...]).astype(BF16)
    qf = jnp.dot(cqn, wuq_ref[...], preferred_element_type=F32)
    ckvn = (ckv * _rms(ckv, KV_LORA_RANK) * kvg_ref[...]).astype(BF16)
    kvf = jnp.dot(ckvn, wukv_ref[...], preferred_element_type=F32)

    kr_ss = jnp.sum(jnp.where(rope_lanes, kr * kr, 0.0), axis=-1, keepdims=True)
    t = kr * kg2_ref[...] * cs
    kro = jnp.where(rope_lanes, t + pltpu.roll(t, QK_ROPE_DIM, 1), 0.0)

    qg = qg_ref[...]
    scale = QK_HEAD_DIM ** -0.5 * math.log2(math.e)
    for hd in range(N_HEADS):
        qn = qf[:, hd * QK_PAD:hd * QK_PAD + LANES]
        q2 = qf[:, hd * QK_PAD + LANES:(hd + 1) * QK_PAD]
        q_ss = (jnp.sum(qn * qn, axis=-1, keepdims=True)
                + jnp.sum(jnp.where(rope_lanes, q2 * q2, 0.0), axis=-1, keepdims=True))
        q_rs = lax.rsqrt(q_ss * (1.0 / QK_HEAD_DIM) + NORM_EPS) * scale
        t = q2 * qg[:, LANES:] * cs
        q_ref[0, hd, :, :LANES] = (qn * qg[:, :LANES] * q_rs).astype(BF16)
        q_ref[0, hd, :, LANES:] = ((t + pltpu.roll(t, QK_ROPE_DIM, 1)) * q_rs).astype(BF16)

        kn = kvf[:, hd * QK_NOPE_DIM:(hd + 1) * QK_NOPE_DIM]
        k_ss = jnp.sum(kn * kn, axis=-1, keepdims=True) + kr_ss
        k_rs = lax.rsqrt(k_ss * (1.0 / QK_HEAD_DIM) + NORM_EPS)
        k_ref[0, hd, :, :LANES] = (kn * kgn_ref[...] * k_rs).astype(BF16)
        k_ref[0, hd, :, LANES:] = (kro * k_rs).astype(BF16)
        v_ref[0, hd] = kvf[:, N_HEADS * QK_NOPE_DIM + hd * V_HEAD_DIM:
                           N_HEADS * QK_NOPE_DIM + (hd + 1) * V_HEAD_DIM].astype(BF16)


def _pre(x2d, cs, w, batch, seq):
    n_tok = batch * seq
    tiles_per_seq = seq // PRE_TM
    tok = lambda i: (i, 0)
    head = lambda i: (i // tiles_per_seq, 0, i % tiles_per_seq, 0)
    in_specs = [
        pl.BlockSpec((PRE_TM, D_MODEL), tok),
        pl.BlockSpec((PRE_TM, LANES), lambda i: (i % tiles_per_seq, 0)),
        _const_spec((1, D_MODEL)),
        _const_spec((D_MODEL, P_COLS)),
        _const_spec((1, A_WIDTH)),
        _const_spec((A_GROUPS, CHUNK, CHUNK)),
        _const_spec((CHUNK, A_GROUPS)),
        _const_spec((A_WIDTH, D_MODEL)),
        _const_spec((1, Q_LORA_RANK)),
        _const_spec((Q_LORA_RANK, N_HEADS * QK_PAD)),
        _const_spec((1, KV_LORA_RANK)),
        _const_spec((KV_LORA_RANK, N_HEADS * (QK_NOPE_DIM + V_HEAD_DIM))),
        _const_spec((1, QK_PAD)),
        _const_spec((1, LANES)),
        _const_spec((1, LANES)),
    ]
    out_specs = [
        pl.BlockSpec((PRE_TM, D_MODEL), tok),
        pl.BlockSpec((PRE_TM, D_MODEL), tok),
        pl.BlockSpec((1, N_HEADS, PRE_TM, QK_PAD), head),
        pl.BlockSpec((1, N_HEADS, PRE_TM, QK_PAD), head),
        pl.BlockSpec((1, N_HEADS, PRE_TM, V_HEAD_DIM), head),
    ]
    out_shape = [
        jax.ShapeDtypeStruct((n_tok, D_MODEL), F32),
        jax.ShapeDtypeStruct((n_tok, D_MODEL), F32),
        jax.ShapeDtypeStruct((batch, N_HEADS, seq, QK_PAD), BF16),
        jax.ShapeDtypeStruct((batch, N_HEADS, seq, QK_PAD), BF16),
        jax.ShapeDtypeStruct((batch, N_HEADS, seq, V_HEAD_DIM), BF16),
    ]
    return pl.pallas_call(
        _pre_kernel,
        grid=(n_tok // PRE_TM,),
        in_specs=in_specs,
        out_specs=out_specs,
        out_shape=out_shape,
        scratch_shapes=[pltpu.VMEM((PRE_TM, A_WIDTH), F32)],
        compiler_params=_cparams(("arbitrary",)),
        name="pre",
    )(x2d, cs, w["g1"], w["w_in"], w["vg"], w["ws"], w["bt"], w["wa"], w["qng"], w["wuq"],
      w["kvg"], w["wukv"], w["qg"], w["kgn"], w["kg2"])


def _attn_kernel(q_ref, k_ref, v_ref, o_ref, m_ref, l_ref, acc_ref, *, seq):
    q = q_ref[0, 0]
    m_ref[...] = jnp.full(m_ref.shape, -jnp.inf, F32)
    l_ref[...] = jnp.zeros(l_ref.shape, F32)
    acc_ref[...] = jnp.zeros(acc_ref.shape, F32)
    n_chunks = ATT_TK // LANES

    def body(j, carry):
        start = pl.multiple_of(j * ATT_TK, ATT_TK)
        kt = k_ref[0, 0, pl.ds(start, ATT_TK), :]
        vt = v_ref[0, 0, pl.ds(start, ATT_TK), :]
        s = lax.dot_general(q, kt, (((1,), (1,)), ((), ())), preferred_element_type=F32)
        chunks = [s[:, c * LANES:(c + 1) * LANES] for c in range(n_chunks)]
        m_old = m_ref[...]
        row_max = jnp.max(functools.reduce(jnp.maximum, chunks), axis=-1, keepdims=True)
        m_new = jnp.maximum(m_old, row_max)
        alpha = jnp.exp2(m_old - m_new)
        ps = [jnp.exp2(c - m_new) for c in chunks]
        l_ref[...] = alpha * l_ref[...] + functools.reduce(jnp.add, ps)
        p = jnp.concatenate([x.astype(BF16) for x in ps], axis=1)
        acc_ref[...] = alpha * acc_ref[...] + jnp.dot(p, vt, preferred_element_type=F32)
        m_ref[...] = m_new
        return carry

    lax.fori_loop(0, seq // ATT_TK, body, 0, unroll=ATT_UNROLL)
    l = jnp.sum(l_ref[...], axis=-1, keepdims=True)
    o_ref[0] = (acc_ref[...] / l).astype(BF16)


def _attn(q, k, v, batch, seq):
    return pl.pallas_call(
        functools.partial(_attn_kernel, seq=seq),
        grid=(batch, N_HEADS, seq // ATT_TQ),
        in_specs=[
            pl.BlockSpec((1, 1, ATT_TQ, QK_PAD), lambda b, h, i: (b, h, i, 0)),
            pl.BlockSpec((1, 1, seq, QK_PAD), lambda b, h, i: (b, h, 0, 0)),
            pl.BlockSpec((1, 1, seq, V_HEAD_DIM), lambda b, h, i: (b, h, 0, 0)),
        ],
        out_specs=pl.BlockSpec((1, ATT_TQ, V_HEAD_DIM), lambda b, h, i: (b, i, h)),
        out_shape=jax.ShapeDtypeStruct((batch, seq, N_HEADS * V_HEAD_DIM), BF16),
        scratch_shapes=[pltpu.VMEM((ATT_TQ, LANES), F32), pltpu.VMEM((ATT_TQ, LANES), F32),
                        pltpu.VMEM((ATT_TQ, V_HEAD_DIM), F32)],
        compiler_params=_cparams(("arbitrary", "arbitrary", "arbitrary")),
        name="attn",
    )(q, k, v)


def _post_kernel(x_ref, o_ref, ab_ref, gb_ref, wb_ref, wo_ref, g2_ref, wr_ref,
                 x1_ref, hn_ref, aff_ref, aff16_ref):
    b = jnp.dot(o_ref[...], wb_ref[...], preferred_element_type=F32)
    mix = ab_ref[...] + gb_ref[...] * b
    x1 = x_ref[...] + jnp.dot(mix.astype(BF16), wo_ref[...], preferred_element_type=F32)
    x1_ref[...] = x1
    hn = (x1 * _rms(x1, D_MODEL) * g2_ref[...]).astype(BF16)
    hn_ref[...] = hn
    logits = jnp.dot(hn, wr_ref[...], preferred_element_type=F32)
    lane = lax.broadcasted_iota(jnp.int32, logits.shape, 1)
    logits = jnp.where(lane < N_EXPERTS, logits, -jnp.inf)
    e = jnp.exp(logits - jnp.max(logits, axis=-1, keepdims=True))
    aff = e / jnp.sum(e, axis=-1, keepdims=True)
    aff_ref[...] = aff
    aff16_ref[...] = aff[:, :N_EXPERTS]


def _post(x2d, o2d, ab, gb, w):
    n_tok = x2d.shape[0]
    tok = lambda i: (i, 0)
    return pl.pallas_call(
        _post_kernel,
        grid=(n_tok // POST_TM,),
        in_specs=[
            pl.BlockSpec((POST_TM, D_MODEL), tok),
            pl.BlockSpec((POST_TM, D_MODEL), tok),
            pl.BlockSpec((POST_TM, D_MODEL), tok),
            pl.BlockSpec((POST_TM, D_MODEL), tok),
            _const_spec((N_HEADS * V_HEAD_DIM, D_MODEL)),
            _const_spec((D_MODEL, D_MODEL)),
            _const_spec((1, D_MODEL)),
            _const_spec((D_MODEL, LANES)),
        ],
        out_specs=[
            pl.BlockSpec((POST_TM, D_MODEL), tok),
            pl.BlockSpec((POST_TM, D_MODEL), tok),
            pl.BlockSpec((POST_TM, LANES), tok),
            pl.BlockSpec((POST_TM, N_EXPERTS), tok),
        ],
        out_shape=[
            jax.ShapeDtypeStruct((n_tok, D_MODEL), F32),
            jax.ShapeDtypeStruct((n_tok, D_MODEL), BF16),
            jax.ShapeDtypeStruct((n_tok, LANES), F32),
            jax.ShapeDtypeStruct((n_tok, N_EXPERTS), F32),
        ],
        compiler_params=_cparams(("arbitrary",)),
        name="post",
    )(x2d, o2d, ab, gb, w["wb"], w["wo"], w["g2"], w["wr"])


def _thresh_kernel(aff_ref, lo_ref, hi_ref, need_ref, *, cap):
    aff = aff_ref[...]

    def per_expert(c):
        for shift in (64, 32, 16):
            c = c + pltpu.roll(c, shift, 1)
        return c

    def count_ge(x):
        return per_expert(jnp.sum(jnp.where(aff >= x, 1.0, 0.0), axis=0, keepdims=True))

    def body(i, carry):
        lo, hi = carry
        m2 = 0.5 * (lo + hi)
        m1 = 0.5 * (lo + m2)
        m3 = 0.5 * (m2 + hi)
        t1 = count_ge(m1) >= cap
        t2 = count_ge(m2) >= cap
        t3 = count_ge(m3) >= cap
        new_lo = jnp.where(t3, m3, jnp.where(t2, m2, jnp.where(t1, m1, lo)))
        new_hi = jnp.where(t1, jnp.where(t2, jnp.where(t3, hi, m3), m2), m1)
        return new_lo, new_hi

    lo, hi = lax.fori_loop(0, BISECT_STEPS, body,
                           (jnp.zeros((1, LANES), F32), jnp.full((1, LANES), 2.0, F32)))
    lo_ref[...] = lo
    hi_ref[...] = hi
    need_ref[...] = cap - count_ge(hi)


def _thresh(aff16, cap):
    n_tok = aff16.shape[0]
    packed = aff16.reshape(n_tok * N_EXPERTS // LANES, LANES)
    row = jax.ShapeDtypeStruct((1, LANES), F32)
    lo, hi, need = pl.pallas_call(
        functools.partial(_thresh_kernel, cap=float(cap)),
        out_shape=[row, row, row],
        compiler_params=pltpu.CompilerParams(vmem_limit_bytes=VMEM_LIMIT_BYTES),
        name="thresh",
    )(packed)
    lane = jnp.arange(LANES)
    lo = jnp.where(lane < N_EXPERTS, lo, jnp.inf)
    hi = jnp.where(lane < N_EXPERTS, hi, jnp.inf)
    need = jnp.where(lane < N_EXPERTS, need, 0.0)
    return lo, hi, need


def _slots_kernel(aff_ref, lo_ref, hi_ref, need_ref, lp_tok_ref, lp_exp_ref, off_ref, run_sel, run_eq):
    @pl.when(pl.program_id(0) == 0)
    def _():
        run_sel[...] = jnp.zeros(run_sel.shape, F32)
        run_eq[...] = jnp.zeros(run_eq.shape, F32)

    lo = lo_ref[...]
    hi = hi_ref[...]
    row = lax.broadcasted_iota(jnp.int32, (TOK_TILE, TOK_TILE), 0)
    col = lax.broadcasted_iota(jnp.int32, (TOK_TILE, TOK_TILE), 1)
    earlier = jnp.where(col < row, 1.0, 0.0).astype(BF16)
    for t in range(SLOTS_TILES):
        aff = aff_ref[t * TOK_TILE:(t + 1) * TOK_TILE, :]
        above = aff >= hi
        eq = (aff >= lo) & jnp.logical_not(above)
        eqf = jnp.where(eq, 1.0, 0.0)
        eq_rank = jnp.dot(earlier, eqf.astype(BF16), preferred_element_type=F32) + run_eq[...]
        sel = above | (eq & (eq_rank < need_ref[...]))
        self_ = jnp.where(sel, 1.0, 0.0)
        lpos = jnp.dot(earlier, self_.astype(BF16), preferred_element_type=F32)
        lp = jnp.where(sel, lpos, -1.0)
        lp_tok_ref[t * TOK_TILE:(t + 1) * TOK_TILE, :] = lp
        lp_exp_ref[t] = lp.T[:N_EXPERTS, :]
        off_ref[t] = run_sel[...]
        run_sel[...] = run_sel[...] + jnp.sum(self_, axis=0, keepdims=True)
        run_eq[...] = run_eq[...] + jnp.sum(eqf, axis=0, keepdims=True)


def _slots(aff, lo, hi, need):
    n_tok = aff.shape[0]
    n_tiles = n_tok // TOK_TILE
    return pl.pallas_call(
        _slots_kernel,
        grid=(n_tiles // SLOTS_TILES,),
        in_specs=[pl.BlockSpec((SLOTS_TILES * TOK_TILE, LANES), lambda i: (i, 0)),
                  _const_spec((1, LANES)), _const_spec((1, LANES)), _const_spec((1, LANES))],
        out_specs=[pl.BlockSpec((SLOTS_TILES * TOK_TILE, LANES), lambda i: (i, 0)),
                   pl.BlockSpec((SLOTS_TILES, N_EXPERTS, TOK_TILE), lambda i: (i, 0, 0)),
                   pl.BlockSpec((SLOTS_TILES, 1, LANES), lambda i: (i, 0, 0))],
        out_shape=[jax.ShapeDtypeStruct((n_tok, LANES), F32),
                   jax.ShapeDtypeStruct((n_tiles, N_EXPERTS, TOK_TILE), F32),
                   jax.ShapeDtypeStruct((n_tiles, 1, LANES), F32)],
        scratch_shapes=[pltpu.VMEM((1, LANES), F32), pltpu.VMEM((1, LANES), F32)],
        compiler_params=_cparams(("arbitrary",)),
        name="slots",
    )(aff, lo, hi, need)


def _gather_kernel(off_ref, lo_ref, hi_ref, lp_ref, hn_hbm, xs_ref, hn_vmem, sem, acc_ref, *,
                   n_blocks, n_tok):
    e = pl.program_id(0)
    j = pl.program_id(1)

    @pl.when((e == 0) & (j == 0))
    def _():
        hn_vmem[pl.ds(n_tok, GATHER_PAD_ROWS), :] = jnp.zeros((GATHER_PAD_ROWS, D_MODEL), BF16)
        load = pltpu.make_async_copy(hn_hbm, hn_vmem.at[pl.ds(0, n_tok), :], sem.at[0])
        load.start()
        load.wait()

    lo = lo_ref[e * n_blocks + j]
    hi = hi_ref[e * n_blocks + j]
    acc_ref[...] = jnp.zeros(acc_ref.shape, F32)
    slot_id = lax.broadcasted_iota(jnp.int32, (SLOT_BLOCK, TOK_TILE), 0).astype(F32)
    base = (j * SLOT_BLOCK).astype(F32)

    def body(i, carry):
        first = lo + i * GATHER_WINDOW
        pieces = []
        for t in range(GATHER_WINDOW):
            c = first + t
            lp = lp_ref[c, pl.ds(e, 1), :]
            dst = lp + (off_ref[c * N_EXPERTS + e].astype(F32) - base)
            pieces.append(jnp.where((slot_id == dst) & (lp >= 0.0), 1.0, 0.0).astype(BF16))
        onehot = jnp.concatenate(pieces, axis=1)
        start = pl.multiple_of(first * TOK_TILE, TOK_TILE)
        rows = hn_vmem[pl.ds(start, GATHER_WINDOW * TOK_TILE), :]
        acc_ref[...] += jnp.dot(onehot, rows, preferred_element_type=F32)
        return carry

    lax.fori_loop(0, (hi - lo + GATHER_WINDOW) // GATHER_WINDOW, body, 0)
    xs_ref[0] = acc_ref[...].astype(BF16)


def _gather(off, lo, hi, lp_exp, hn, cap):
    n_tok = hn.shape[0]
    n_tiles = lp_exp.shape[0]
    n_blocks = cap // SLOT_BLOCK
    pad = GATHER_WINDOW - 1
    lp_pad = jnp.pad(lp_exp, ((0, pad), (0, 0), (0, 0)), constant_values=-1.0)
    off_pad = jnp.pad(off, ((0, pad), (0, 0))).reshape(-1)
    grid_spec = pltpu.PrefetchScalarGridSpec(
        num_scalar_prefetch=3,
        grid=(N_EXPERTS, n_blocks),
        in_specs=[pl.BlockSpec((n_tiles + pad, N_EXPERTS, TOK_TILE), lambda e, j, *_: (0, 0, 0)),
                  pl.BlockSpec(memory_space=pl.ANY)],
        out_specs=pl.BlockSpec((1, SLOT_BLOCK, D_MODEL), lambda e, j, *_: (e, j, 0)),
        scratch_shapes=[pltpu.VMEM((n_tok + GATHER_PAD_ROWS, D_MODEL), BF16),
                        pltpu.SemaphoreType.DMA((1,)),
                        pltpu.VMEM((SLOT_BLOCK, D_MODEL), F32)],
    )
    return pl.pallas_call(
        functools.partial(_gather_kernel, n_blocks=n_blocks, n_tok=n_tok),
        grid_spec=grid_spec,
        out_shape=jax.ShapeDtypeStruct((N_EXPERTS, cap, D_MODEL), BF16),
        compiler_params=_cparams(("arbitrary", "arbitrary")),
        name="gather",
    )(off_pad, lo, hi, lp_pad, hn)


def _ffn_kernel(xs_ref, wg_ref, wu_ref, wd_ref, eo_ref):
    xs = xs_ref[0]
    g = jnp.dot(xs, wg_ref[0], preferred_element_type=F32)
    u = jnp.dot(xs, wu_ref[0], preferred_element_type=F32)
    hid = (g * jax.nn.sigmoid(g) * u).astype(BF16)
    eo_ref[0] = jnp.dot(hid, wd_ref[0], preferred_element_type=F32).astype(BF16)


def _ffn(xs, w):
    cap = xs.shape[1]
    wspec = lambda shape: pl.BlockSpec(shape, lambda e, j: (e, 0, 0))
    return pl.pallas_call(
        _ffn_kernel,
        grid=(N_EXPERTS, cap // FFN_TM),
        in_specs=[pl.BlockSpec((1, FFN_TM, D_MODEL), lambda e, j: (e, j, 0)),
                  wspec((1, D_MODEL, EXPERT_FF)), wspec((1, D_MODEL, EXPERT_FF)),
                  wspec((1, EXPERT_FF, D_MODEL))],
        out_specs=pl.BlockSpec((1, FFN_TM, D_MODEL), lambda e, j: (e, j, 0)),
        out_shape=jax.ShapeDtypeStruct((N_EXPERTS, cap, D_MODEL), BF16),
        compiler_params=_cparams(("arbitrary", "arbitrary")),
        name="ffn",
    )(xs, w["wg"], w["wu"], w["wd"])


def _combine_kernel(off_ref, x1_ref, aff_ref, lp_ref, eo_hbm, y_ref, buf, sem, *, cap, n_tiles):
    i = pl.program_id(0)

    def window_start(tile, e):
        off = off_ref[tile * N_EXPERTS + e]
        aligned = (off // BF16_SUBLANES) * BF16_SUBLANES
        return pl.multiple_of(jnp.minimum(aligned, cap - READ_WINDOW), BF16_SUBLANES)

    def window_copy(tile, e, slot):
        return pltpu.make_async_copy(
            eo_hbm.at[e, pl.ds(window_start(tile, e), READ_WINDOW), :],
            buf.at[slot, e, pl.ds(0, READ_WINDOW), :], sem.at[slot, e])

    @pl.when(i == 0)
    def _():
        buf[:, :, READ_WINDOW:, :] = jnp.zeros(
            (2, N_EXPERTS, 2 * TOK_TILE - READ_WINDOW, D_MODEL), BF16)
        for e in range(N_EXPERTS):
            window_copy(0, e, 0).start()

    slot = i % 2

    @pl.when(i + 1 < n_tiles)
    def _():
        for e in range(N_EXPERTS):
            window_copy(i + 1, e, 1 - slot).start()

    lp = lp_ref[...]
    aff = aff_ref[...]
    lane = lax.broadcasted_iota(jnp.int32, (TOK_TILE, 2 * TOK_TILE), 1).astype(F32)
    for e in range(N_EXPERTS):
        window_copy(i, e, slot).wait()
    acc = x1_ref[...]
    for e in range(N_EXPERTS):
        shift =(off_ref[i * N_EXPERTS + e] - window_start(i, e)).astype(F32)
        lpe = lp[:, e:e + 1]
        onehot = jnp.where((lane == lpe + shift) & (lpe >= 0.0), 1.0, 0.0).astype(BF16)
        rows = jnp.dot(onehot, buf[slot, e], preferred_element_type=F32)
        acc = acc + aff[:, e:e + 1] * rows
    y_ref[...] = acc


def _combine(off_flat, x1, aff, lp_tok, eo):
    n_tok = x1.shape[0]
    n_tiles = n_tok // TOK_TILE
    cap = eo.shape[1]
    grid_spec = pltpu.PrefetchScalarGridSpec(
        num_scalar_prefetch=1,
        grid=(n_tiles,),
        in_specs=[pl.BlockSpec((TOK_TILE, D_MODEL), lambda i, *_: (i, 0)),
                  pl.BlockSpec((TOK_TILE, LANES), lambda i, *_: (i, 0)),
                  pl.BlockSpec((TOK_TILE, LANES), lambda i, *_: (i, 0)),
                  pl.BlockSpec(memory_space=pl.ANY)],
        out_specs=pl.BlockSpec((TOK_TILE, D_MODEL), lambda i, *_: (i, 0)),
        scratch_shapes=[pltpu.VMEM((2, N_EXPERTS, 2 * TOK_TILE, D_MODEL), BF16),
                        pltpu.SemaphoreType.DMA((2, N_EXPERTS))],
    )
    return pl.pallas_call(
        functools.partial(_combine_kernel, cap=cap, n_tiles=n_tiles),
        grid_spec=grid_spec,
        out_shape=jax.ShapeDtypeStruct((n_tok, D_MODEL), F32),
        compiler_params=_cparams(("arbitrary",)),
        name="combine",
    )(off_flat, x1, aff, lp_tok, eo)


def _rot_cols(w):
    return jnp.concatenate([-w[..., HALF_ROPE:], w[..., :HALF_ROPE]], axis=-1)


def _swap_halves(g):
    return jnp.concatenate([g[HALF_ROPE:], g[:HALF_ROPE]])


def _prep_weights(norm1_g, w_in, v_norm_g, w_spatial, b_spatial, w_a_out, q_norm_g, w_uq,
                  kv_norm_g, w_ukv, q_head_g, k_head_g, w_b_out, w_out, norm2_g,
                  w_router, w_gate, w_up, w_down):
    off_cq = 2 * A_WIDTH
    off_kr = off_cq + Q_LORA_RANK + KV_LORA_RANK
    off_ga = off_kr + QK_ROPE_DIM
    kr = w_in[:, off_kr:off_ga]
    w_in_p = jnp.concatenate([w_in[:, :off_ga], _rot_cols(kr), w_in[:, off_ga:]], axis=1)
    uq = w_uq.reshape(Q_LORA_RANK, N_HEADS, QK_HEAD_DIM)
    uq_rope = uq[..., QK_NOPE_DIM:]
    wuq = jnp.concatenate([uq, _rot_cols(uq_rope)], axis=-1).reshape(Q_LORA_RANK, N_HEADS * QK_PAD)
    ukv = w_ukv.reshape(KV_LORA_RANK, N_HEADS, QK_NOPE_DIM + V_HEAD_DIM)
    wukv = jnp.concatenate([ukv[..., :QK_NOPE_DIM].reshape(KV_LORA_RANK, -1),
                            ukv[..., QK_NOPE_DIM:].reshape(KV_LORA_RANK, -1)], axis=1)
    qg_r = q_head_g[QK_NOPE_DIM:]
    kg_r = k_head_g[QK_NOPE_DIM:]
    return {
        "g1": norm1_g.reshape(1, -1),
        "w_in": w_in_p.astype(BF16),
        "vg": v_norm_g.reshape(1, -1),
        "ws": w_spatial.astype(BF16),
        "bt": b_spatial.T,
        "wa": w_a_out.astype(BF16),
        "qng": q_norm_g.reshape(1, -1),
        "wuq": wuq.astype(BF16),
        "kvg": kv_norm_g.reshape(1, -1),
        "wukv": wukv.astype(BF16),
        "qg": jnp.concatenate([q_head_g, _swap_halves(qg_r)]).reshape(1, -1),
        "kgn": k_head_g[:QK_NOPE_DIM].reshape(1, -1),
        "kg2": jnp.concatenate([kg_r, _swap_halves(kg_r)]).reshape(1, -1),
        "wb": w_b_out.astype(BF16),
        "wo": w_out.astype(BF16),
        "g2": norm2_g.reshape(1, -1),
        "wr": jnp.pad(w_router, ((0, 0), (0, LANES - N_EXPERTS))).astype(BF16),
        "wg": w_gate.astype(BF16),
        "wu": w_up.astype(BF16),
        "wd": w_down.astype(BF16),
    }


def _rope_table(seq):
    inv_freq = ROPE_THETA ** (-jnp.arange(HALF_ROPE, dtype=F32) / HALF_ROPE)
    freqs = jnp.arange(seq, dtype=F32)[:, None] * inv_freq[None, :]
    cos, sin = jnp.cos(freqs), jnp.sin(freqs)
    return jnp.concatenate([cos, cos, sin, sin], axis=1)


def _block_tile_ranges(off, cap):
    n_blocks = cap // SLOT_BLOCK
    off_end = jnp.concatenate([off[1:], jnp.full((1, N_EXPERTS), cap, jnp.int32)], axis=0)
    block_start = jnp.arange(n_blocks, dtype=jnp.int32) * SLOT_BLOCK
    lo = jnp.sum(off_end[:, :, None] <= block_start[None, None, :], axis=0)
    hi = jnp.sum(off[:, :, None] < (block_start + SLOT_BLOCK)[None, None, :], axis=0) - 1
    return lo.astype(jnp.int32).reshape(-1), hi.astype(jnp.int32).reshape(-1)


def _layer_group(x, w):
    batch, seq, _ = x.shape
    n_tok = batch * seq
    cap = max(1, min(n_tok, CAPACITY_FACTOR * n_tok // N_EXPERTS))
    x2d = x.reshape(n_tok, D_MODEL)
    ab, gb, q, k, v = _pre(x2d, _rope_table(seq), w, batch, seq)
    o = _attn(q, k, v, batch, seq)
    x1, hn, aff, aff16 = _post(x2d, o.reshape(n_tok, D_MODEL), ab, gb, w)
    lo_thr, hi_thr, need = _thresh(aff16, cap)
    lp_tok, lp_exp, off = _slots(aff, lo_thr, hi_thr, need)
    off = off[:, 0, :N_EXPERTS].astype(jnp.int32)
    lo, hi = _block_tile_ranges(off, cap)
    off_flat = off.reshape(-1)
    xs = _gather(off, lo, hi, lp_exp, hn, cap)
    eo = _ffn(xs, w)
    y = _combine(off_flat, x1, aff, lp_tok, eo)
    return y.reshape(batch, seq, D_MODEL)


def kernel(x_prompt, x_sample, norm1_g, w_in, v_norm_g, w_spatial, b_spatial, w_a_out, q_norm_g,
           w_uq, kv_norm_g, w_ukv, q_head_g, k_head_g, w_b_out, w_out, norm2_g, w_router,
           w_gate, w_up, w_down):
    params = (norm1_g, w_in, v_norm_g, w_spatial, b_spatial, w_a_out, q_norm_g, w_uq, kv_norm_g,
              w_ukv, q_head_g, k_head_g, w_b_out, w_out, norm2_g, w_router, w_gate, w_up, w_down)
    depth = norm1_g.shape[0]
    y_prompt, y_sample = x_prompt, x_sample
    for l in range(depth):
        w = _prep_weights(*[p[l] for p in params])
        y_prompt = _layer_group(y_prompt, w)
        y_sample = _layer_group(y_sample, w)
    return (y_prompt, y_sample)
```
